```python
import jax, jax.numpy as jnp
from jax import lax
import numpy as np

D_MODEL = 1024
BATCH = 8
SEQ = 2048
DEPTH = 1
DEC_BATCH = 32
DEC_SEQ = 8
PAST_LEN = 16384
PAGE_SIZE = 128

HEAD_DIM = 64
H_A = 8
W_A = H_A * HEAD_DIM
MOBA_BLOCK = 256
MOBA_TOPK = 3
H_B = 8
KV_B = 2
GROUP_B = H_B // KV_B
W_B = H_B * HEAD_DIM
W_KVB = KV_B * HEAD_DIM
CMP_LEN = 32
CMP_STRIDE = 16
CMP_HID = 256
SEL_BLOCK = 64
SEL_TOPN = 16
SEL_LOCAL = 2
WINDOW = 512
D_FF = -(-8 * D_MODEL // (3 * 256)) * 256
Q_CHUNK = 128
N_ADA = 6
EPS = 1e-6
NEG_INF = -1e30
FORCE_SCORE = 1e9
_SPLIT_SIZES = (W_A, W_A, W_A, W_B) + (W_KVB,) * 6 + (H_B * 3, D_MODEL, D_MODEL)
SPLIT_AT = tuple(int(v) for v in np.cumsum(_SPLIT_SIZES)[:-1])
IN_WIDTH = int(sum(_SPLIT_SIZES))

kernel_name = 'moba_nsa_gated_parallel_decoder_step'


def _rmsnorm(x, g):
    xf = x.astype(jnp.float32)
    y = xf * lax.rsqrt(jnp.mean(xf * xf, axis=-1, keepdims=True) + EPS)
    return (y * g.astype(jnp.float32)).astype(x.dtype)


def _alibi_slopes(n):
    return jnp.exp2(-8.0 * (jnp.arange(n, dtype=jnp.float32) + 1.0) / n)


def _masked_softmax(s, mask, axis):
    s = jnp.where(mask, s, NEG_INF)
    e = jnp.where(mask, jnp.exp(s - jnp.max(s, axis=axis, keepdims=True)), 0.0)
    return e / jnp.maximum(jnp.sum(e, axis=axis, keepdims=True), 1e-30)


def _gathered_block_attn(q, k, v, idx, ok, offset, block, slopes):
    B, T, KV, G, Dh = q.shape
    L = k.shape[1]
    nb = -(-L // block)
    pad = ((0, 0), (0, nb * block - L), (0, 0), (0, 0))
    kb = jnp.pad(k, pad).reshape(B, nb, block, KV, Dh).transpose(0, 1, 3, 2, 4)
    vb = jnp.pad(v, pad).reshape(B, nb, block, KV, Dh).transpose(0, 1, 3, 2, 4)
    n = idx.shape[-1]
    qc = min(Q_CHUNK, T)
    nc = T // qc
    kv_ar = jnp.arange(KV)[None, :, None]
    inner = jnp.arange(block)
    scale = Dh ** -0.5

    def one(args):
        q_c, i_c, ok_c, b, p0 = args
        ks = kb[b, i_c, kv_ar]
        vs = vb[b, i_c, kv_ar]
        qpos = (p0 + jnp.arange(qc))[:, None, None, None]
        kpos = i_c[..., None] * block + inner
        s = jnp.einsum('qkgd,qknsd->qkgns', q_c.astype(jnp.float32), ks.astype(jnp.float32)) * scale
        s = s - slopes[None, :, :, None, None] * (qpos - kpos).astype(jnp.float32)[:, :, None]
        mask = (ok_c[..., None] & (kpos <= qpos))[:, :, None]
        p = _masked_softmax(s, mask, (-2, -1))
        return jnp.einsum('qkgns,qknsd->qkgd', p, vs.astype(jnp.float32)).astype(q_c.dtype)

    step = jnp.arange(B * nc)
    xs = (q.reshape(B * nc, qc, KV, G, Dh), idx.reshape(B * nc, qc, KV, n),
          ok.reshape(B * nc, qc, KV, n), step // nc, offset + (step % nc) * qc)
    return lax.map(one, xs).reshape(B, T, KV, G, Dh)


def _moba(q, k, v, offset, slopes):
    B, T, H = q.shape[:3]
    L = k.shape[1]
    nb = -(-L // MOBA_BLOCK)
    kp = jnp.pad(k, ((0, 0), (0, nb * MOBA_BLOCK - L), (0, 0), (0, 0)))
    kmean = jnp.mean(kp.reshape(B, nb, MOBA_BLOCK, H, HEAD_DIM).astype(jnp.float32), axis=2)
    own = ((offset + jnp.arange(T)) // MOBA_BLOCK)[None, :, None, None]
    gate = jnp.einsum('bthd,bnhd->bthn', q[:, :, :, 0].astype(jnp.float32), kmean)
    gate = jnp.where(jnp.arange(nb) < own, gate, NEG_INF)
    _, top = lax.top_k(gate, min(MOBA_TOPK, nb))
    idx = jnp.concatenate([top, jnp.broadcast_to(own, (B, T, H, 1))], axis=-1)
    ok = jnp.concatenate([top < own, jnp.ones((B, T, H, 1), bool)], axis=-1)
    return _gathered_block_attn(q, k, v, idx, ok, offset, MOBA_BLOCK, slopes.reshape(H, 1))


def _compress(k, pe, w1, b1, w2, b2):
    B, L, KV, Dh = k.shape
    r = CMP_LEN // CMP_STRIDE
    nseg = -(-L // CMP_STRIDE)
    kp = jnp.pad(k, ((0, 0), (0, nseg * CMP_STRIDE - L), (0, 0), (0, 0)))
    seg = kp.reshape(B, nseg, CMP_STRIDE, KV, Dh)
    n_cmp = nseg - r + 1
    blk = jnp.concatenate([seg[:, j:j + n_cmp] for j in range(r)], axis=2)
    blk = blk + pe[:, None, :]
    flat = jnp.swapaxes(blk, 2, 3).reshape(B, n_cmp, KV, CMP_LEN * Dh)
    return jax.nn.gelu(flat @ w1 + b1) @ w2 + b2


def _nsa_cmp_sel(q, k_cmp, v_cmp, k_sel, v_sel, offset, w, slopes):
    B, T = q.shape[:2]
    L = k_cmp.shape[1]
    qpos = offset + jnp.arange(T)
    ck = _compress(k_cmp, w['cmp_pe'][0], w['cmp_w1'][0], w['cmp_b1'][0], w['cmp_w2'][0], w['cmp_b2'][0])
    cv = _compress(v_cmp, w['cmp_pe'][1], w['cmp_w1'][1], w['cmp_b1'][1], w['cmp_w2'][1], w['cmp_b2'][1])
    n_cmp = ck.shape[1]
    c_end = jnp.arange(n_cmp) * CMP_STRIDE + (CMP_LEN - 1)
    s = jnp.einsum('btkgd,bnkd->btkgn', q.astype(jnp.float32), ck.astype(jnp.float32)) * HEAD_DIM ** -0.5
    p_cmp = _masked_softmax(s, (c_end[None, :] <= qpos[:, None])[None, :, None, None, :], -1)
    o_cmp = jnp.einsum('btkgn,bnkd->btkgd', p_cmp, cv.astype(jnp.float32)).astype(q.dtype)
    nsb = -(-L // SEL_BLOCK)
    c_start = jnp.arange(n_cmp)[:, None] * CMP_STRIDE
    s_start = jnp.arange(nsb)[None, :] * SEL_BLOCK
    overlap = ((c_start < s_start + SEL_BLOCK) & (c_start + CMP_LEN > s_start)).astype(jnp.float32)
    imp = jnp.einsum('btkgn,nj->btkj', p_cmp, overlap)
    cur = (qpos // SEL_BLOCK)[None, :, None, None]
    jr = jnp.arange(nsb)
    forced = (jr == 0) | ((jr <= cur) & (jr > cur - SEL_LOCAL))
    imp = jnp.where(jr <= cur, jnp.where(forced, FORCE_SCORE, imp), NEG_INF)
    _, idx = lax.top_k(imp, min(SEL_TOPN, nsb))
    o_sel = _gathered_block_attn(q, k_sel, v_sel, idx, idx <= cur, offset, SEL_BLOCK,
                                 slopes.reshape(KV_B, GROUP_B))
    return o_cmp, o_sel


def _window_attn(q, k, v, qpos, kpos, slopes):
    s = jnp.einsum('btkgd,bskd->btkgs', q.astype(jnp.float32), k.astype(jnp.float32)) * HEAD_DIM ** -0.5
    dist = (qpos[:, None] - kpos[None, :]).astype(jnp.float32)
    s = s - slopes[None, None, :, :, None] * dist[None, :, None, None, :]
    mask = (kpos[None, :] <= qpos[:, None]) & (kpos[None, :] > qpos[:, None] - WINDOW) & (kpos[None, :] >= 0)
    p = _masked_softmax(s, mask[None, :, None, None, :], -1)
    return jnp.einsum('btkgs,bskd->btkgd', p, v.astype(jnp.float32)).astype(q.dtype)


def _window_banded(q, k, v, slopes):
    B, T = q.shape[:2]
    qc = min(Q_CHUNK, T)
    nc = T // qc
    pad = ((0, 0), (WINDOW, 0), (0, 0), (0, 0))
    kp = jnp.pad(k, pad)
    vp = jnp.pad(v, pad)

    def one(c):
        t0 = c * qc
        q_c = lax.dynamic_slice_in_dim(q, t0, qc, axis=1)
        k_c = lax.dynamic_slice_in_dim(kp, t0, WINDOW + qc, axis=1)
        v_c = lax.dynamic_slice_in_dim(vp, t0, WINDOW + qc, axis=1)
        return _window_attn(q_c, k_c, v_c, t0 + jnp.arange(qc), t0 - WINDOW + jnp.arange(WINDOW + qc), slopes)

    o = lax.map(one, jnp.arange(nc))
    return jnp.moveaxis(o, 0, 1).reshape(q.shape)


def _layer_inputs(x, c, w):
    B, T, _ = x.shape
    mod = jax.nn.silu(c) @ w['w_ada'] + w['b_ada']
    shift1, scale1, gate1, shift2, scale2, gate2 = jnp.split(mod[:, None, :], N_ADA, axis=-1)
    u = _rmsnorm(x, w['g_attn']) * (1.0 + scale1) + shift1
    qa, ka, va, qb, kc, vc, ks, vs, kw, vw, gn, ga, gb = jnp.split(u @ w['w_in'], SPLIT_AT, axis=-1)
    kvh = lambda t: t.reshape(B, T, KV_B, HEAD_DIM)
    m = {
        'q_a': qa.reshape(B, T, H_A, 1, HEAD_DIM),
        'k_a': ka.reshape(B, T, H_A, HEAD_DIM),
        'v_a': va.reshape(B, T, H_A, HEAD_DIM),
        'q_b': qb.reshape(B, T, KV_B, GROUP_B, HEAD_DIM),
        'k_cmp': kvh(kc), 'v_cmp': kvh(vc),
        'k_sel': kvh(ks), 'v_sel': kvh(vs),
        'k_win': kvh(kw), 'v_win': kvh(vw),
        'g_nsa': jax.nn.sigmoid(gn).reshape(B, T, KV_B, GROUP_B, 3),
        'g_a': jax.nn.sigmoid(ga),
        'g_b': jax.nn.sigmoid(gb),
    }
    return m, (gate1, shift2, scale2, gate2)


def _layer_output(x, o_a, o_cmp, o_sel, o_win, m, ada, w):
    B, T, _ = x.shape
    gate1, shift2, scale2, gate2 = ada
    g = m['g_nsa']
    o_b = g[..., 0:1] * o_cmp + g[..., 1:2] * o_sel + g[..., 2:3] * o_win
    y_a = o_a.reshape(B, T, W_A) @ w['w_pa']
    y_b = o_b.reshape(B, T, W_B) @ w['w_pb']
    mixed = (m['g_a'] * y_a + m['g_b'] * y_b) @ w['w_out']
    x = x + gate1 * mixed
    h = _rmsnorm(x, w['g_ffn']) * (1.0 + scale2) + shift2
    a, b = jnp.split(h @ w['w_gu'], 2, axis=-1)
    return x + gate2 * ((jax.nn.silu(a) * b) @ w['w_down'])


def _prompt_layer(x, c, w, slopes_a, slopes_b):
    B, T, _ = x.shape
    m, ada = _layer_inputs(x, c, w)
    o_a = _moba(m['q_a'], m['k_a'], m['v_a'], 0, slopes_a)
    o_cmp, o_sel = _nsa_cmp_sel(m['q_b'], m['k_cmp'], m['v_cmp'], m['k_sel'], m['v_sel'], 0, w, slopes_b)
    o_win = _window_banded(m['q_b'], m['k_win'], m['v_win'], slopes_b.reshape(KV_B, GROUP_B))
    y = _layer_output(x, o_a, o_cmp, o_sel, o_win, m, ada, w)
    moba_kv = jnp.stack([m['k_a'], m['v_a']], axis=2)
    nsa_kv = jnp.stack([m['k_cmp'], m['v_cmp'], m['k_sel'], m['v_sel']], axis=2)
    keep = min(WINDOW, T)
    nsa_win = jnp.stack([m['k_win'], m['v_win']], axis=2)[:, T - keep:]
    return y, moba_kv, nsa_kv, nsa_win


def _past_rows(cache, page_table, slot):
    rows = cache[page_table, :, slot]
    DB, n_pages, page = rows.shape[:3]
    return rows.reshape(DB, n_pages * page, rows.shape[3], rows.shape[4])


def _sample_layer(x, c, cache_moba_kv, cache_nsa_kv, cache_nsa_win, page_table, w, slopes_a, slopes_b):
    DB, T, _ = x.shape
    past = page_table.shape[1] * cache_moba_kv.shape[1]
    n_buf = cache_nsa_win.shape[1]
    m, ada = _layer_inputs(x, c, w)
    full = lambda cache, slot, new: jnp.concatenate([_past_rows(cache, page_table, slot), new], axis=1)
    o_a = _moba(m['q_a'], full(cache_moba_kv, 0, m['k_a']), full(cache_moba_kv, 1, m['v_a']), past, slopes_a)
    o_cmp, o_sel = _nsa_cmp_sel(m['q_b'], full(cache_nsa_kv, 0, m['k_cmp']), full(cache_nsa_kv, 1, m['v_cmp']),
                                full(cache_nsa_kv, 2, m['k_sel']), full(cache_nsa_kv, 3, m['v_sel']),
                                past, w, slopes_b)
    win_all = jnp.concatenate([cache_nsa_win, jnp.stack([m['k_win'], m['v_win']], axis=2)], axis=1)
    o_win = _window_attn(m['q_b'], win_all[:, :, 0], win_all[:, :, 1], past + jnp.arange(T),
                         past - n_buf + jnp.arange(n_buf + T), slopes_b.reshape(KV_B, GROUP_B))
    y = _layer_output(x, o_a, o_cmp, o_sel, o_win, m, ada, w)
    moba_kv = jnp.stack([m['k_a'], m['v_a']], axis=2)
    nsa_kv = jnp.stack([m['k_cmp'], m['v_cmp'], m['k_sel'], m['v_sel']], axis=2)
    return y, moba_kv, nsa_kv, win_all[:, T:]


def setup_inputs(seed: int = 0) -> dict:
    key = jax.random.key(seed)
    ks = jax.random.split(key, 24)
    n_pages = PAST_LEN // PAGE_SIZE
    n_pool = 5 * DEC_BATCH * n_pages // 4
    n_buf = min(WINDOW, PAST_LEN)

    def nrm(k, shape, scale):
        return jax.random.normal(k, shape, jnp.float32) * scale

    page_table = jax.random.permutation(ks[0], n_pool)[:DEC_BATCH * n_pages].reshape(DEC_BATCH, n_pages).astype(jnp.int32)
    return {
        'x_prompt': nrm(ks[1], (BATCH, SEQ, D_MODEL), 1.0),
        'x_sample': nrm(ks[2], (DEC_BATCH, DEC_SEQ, D_MODEL), 1.0),
        'cache_moba_kv': nrm(ks[3], (n_pool, PAGE_SIZE, 2, H_A, HEAD_DIM), 1.0),
        'cache_nsa_kv': nrm(ks[4], (n_pool, PAGE_SIZE, 4, KV_B, HEAD_DIM), 1.0),
        'cache_nsa_win': nrm(ks[5], (DEC_BATCH, n_buf, 2, KV_B, HEAD_DIM), 1.0),
        'page_table': page_table,
        'c_prompt': nrm(ks[6], (BATCH, D_MODEL), 1.0),
        'c_sample': nrm(ks[7], (DEC_BATCH, D_MODEL), 1.0),
        'w_ada': nrm(ks[8], (D_MODEL, N_ADA * D_MODEL), 0.5 * D_MODEL ** -0.5),
        'b_ada': nrm(ks[9], (N_ADA * D_MODEL,), 0.02),
        'g_attn': 1.0 + nrm(ks[10], (D_MODEL,), 0.02),
        'w_in': nrm(ks[11], (D_MODEL, IN_WIDTH), D_MODEL ** -0.5),
        'cmp_pe': nrm(ks[12], (2, CMP_LEN, HEAD_DIM), 0.1),
        'cmp_w1': nrm(ks[13], (2, CMP_LEN * HEAD_DIM, CMP_HID), (CMP_LEN * HEAD_DIM) ** -0.5),
        'cmp_b1': nrm(ks[14], (2, CMP_HID), 0.02),
        'cmp_w2': nrm(ks[15], (2, CMP_HID, HEAD_DIM), CMP_HID ** -0.5),
        'cmp_b2': nrm(ks[16], (2, HEAD_DIM), 0.02),
        'w_pa': nrm(ks[17], (W_A, D_MODEL), W_A ** -0.5),
        'w_pb': nrm(ks[18], (W_B, D_MODEL), W_B ** -0.5),
        'w_out': nrm(ks[19], (D_MODEL, D_MODEL), D_MODEL ** -0.5),
        'g_ffn': 1.0 + nrm(ks[20], (D_MODEL,), 0.02),
        'w_gu': nrm(ks[21], (D_MODEL, 2 * D_FF), D_MODEL ** -0.5),
        'w_down': nrm(ks[22], (D_FF, D_MODEL), D_FF ** -0.5),
        'g_final': 1.0 + nrm(ks[23], (D_MODEL,), 0.02),
    }


def reference(x_prompt, x_sample, cache_moba_kv, cache_nsa_kv, cache_nsa_win, page_table, c_prompt, c_sample,
              w_ada, b_ada, g_attn, w_in, cmp_pe, cmp_w1, cmp_b1, cmp_w2, cmp_b2, w_pa, w_pb, w_out,
              g_ffn, w_gu, w_down, g_final):
    w = {'w_ada': w_ada, 'b_ada': b_ada, 'g_attn': g_attn, 'w_in': w_in, 'cmp_pe': cmp_pe,
         'cmp_w1': cmp_w1, 'cmp_b1': cmp_b1, 'cmp_w2': cmp_w2, 'cmp_b2': cmp_b2, 'w_pa': w_pa,
         'w_pb': w_pb, 'w_out': w_out, 'g_ffn': g_ffn, 'w_gu': w_gu, 'w_down': w_down}
    slopes_a = _alibi_slopes(H_A)
    slopes_b = _alibi_slopes(H_B)
    xp, xs = x_prompt, x_sample
    for _ in range(DEPTH):
        xp, moba_kv_prompt, nsa_kv_prompt, nsa_win_prompt = _prompt_layer(xp, c_prompt, w, slopes_a, slopes_b)
        xs, moba_kv_sample, nsa_kv_sample, nsa_win_sample = _sample_layer(
            xs, c_sample, cache_moba_kv, cache_nsa_kv, cache_nsa_win, page_table, w, slopes_a, slopes_b)
    y_prompt = _rmsnorm(xp, g_final)
    y_sample = _rmsnorm(xs, g_final)
    return (y_prompt, y_sample, moba_kv_prompt, nsa_kv_prompt, nsa_win_prompt,
            moba_kv_sample, nsa_kv_sample, nsa_win_sample)
```

```python
import functools

import numpy as np
import jax
import jax.numpy as jnp
from jax import lax
from jax.experimental import pallas as pl
from jax.experimental.pallas import tpu as pltpu

F32 = jnp.float32
BF16 = jnp.bfloat16

D_MODEL = 1024
HEAD_DIM = 64
H_A = 8
W_A = H_A * HEAD_DIM
MOBA_BLOCK = 256
MOBA_TOPK = 3
H_B = 8
KV_B = 2
GROUP_B = H_B // KV_B
W_B = H_B * HEAD_DIM
W_KVB = KV_B * HEAD_DIM
CMP_LEN = 32
CMP_STRIDE = 16
CMP_HID = 256
SEL_BLOCK = 64
SEL_TOPN = 16
SEL_LOCAL = 2
WINDOW = 512
D_FF = -(-8 * D_MODEL // (3 * 256)) * 256
N_ADA = 6
EPS = 1e-6
NEG_INF = -1e30
FORCE_SCORE = 1e9
REMOVED = -3.0e38
SCALE = HEAD_DIM ** -0.5
LANES = 128
GN_PAD = LANES
VMEM_LIMIT = 56 * 1024 * 1024

_C_QA, _C_MKV, _C_QB, _C_NKV, _C_WIN, _C_GA, _C_GB, _C_GN, _C_END = (
    0, 512, 1536, 2048, 2560, 2816, 3840, 4864, 4992)


def _sigmoid(x):
    return 1.0 / (1.0 + jnp.exp(-x))


def _rms(x, g):
    return x * lax.rsqrt(jnp.mean(x * x, axis=-1, keepdims=True) + EPS) * g


def _dot(a, b):
    return jnp.dot(a, b, preferred_element_type=F32)


def _dot_nt(a, b):
    return lax.dot_general(a, b, (((1,), (1,)), ((), ())), preferred_element_type=F32)


def _split(x):
    hi = x.astype(BF16)
    lo = (x - hi.astype(F32)).astype(BF16)
    return hi, lo


def _dot_nt_hilo(a, b):
    ah, al = _split(a)
    bh, bl = _split(b)
    return _dot_nt(ah, bh) + (_dot_nt(ah, bl) + _dot_nt(al, bh))


def _softmax_parts(s, mask):
    s = jnp.where(mask, s, NEG_INF)
    m = jnp.max(s, axis=-1, keepdims=True)
    e = jnp.where(mask, jnp.exp(s - m), 0.0)
    l = jnp.maximum(jnp.sum(e, axis=-1, keepdims=True), 1e-30)
    return e, l


def _topk_pick(score, k, lane_f):
    big = float(score.shape[-1])
    picked = jnp.zeros(score.shape, F32)
    g = score
    for _ in range(k):
        mx = jnp.max(g, axis=-1, keepdims=True)
        idx = jnp.min(jnp.where(g == mx, lane_f, big), axis=-1, keepdims=True)
        pick = lane_f == idx
        picked = jnp.where(pick, 1.0, picked)
        g = jnp.where(pick, REMOVED, g)
    return picked


def _gelu_tanh(x):
    return 0.5 * x * (1.0 + jnp.tanh(np.sqrt(2.0 / np.pi) * (x + 0.044715 * (x * x * x))))


def _const_spec(shape):
    nd = len(shape)
    return pl.BlockSpec(shape, lambda *_: (0,) * nd, pipeline_mode=pl.Buffered(1))


def _params(sem):
    return pltpu.CompilerParams(dimension_semantics=sem, vmem_limit_bytes=VMEM_LIMIT)


def _ada_kernel(c_ref, w_ref, b_ref, o_ref):
    c = c_ref[...]
    s = (c * _sigmoid(c)).astype(BF16)
    o_ref[...] = _dot(s, w_ref[...]) + b_ref[...]


def _ada(c, w_bf, b):
    n, d = c.shape
    nout = w_bf.shape[1]
    tn = 1024
    return pl.pallas_call(
        _ada_kernel,
        out_shape=jax.ShapeDtypeStruct((n, nout), F32),
        grid=(nout // tn,),
        in_specs=[pl.BlockSpec((n, d), lambda j: (0, 0)),
                  pl.BlockSpec((d, tn), lambda j: (0, j)),
                  pl.BlockSpec((1, tn), lambda j: (0, j))],
        out_specs=pl.BlockSpec((n, tn), lambda j: (0, j)),
        compiler_params=_params(("arbitrary",)),
        name="ada_mod",
    )(c, w_bf, b.reshape(1, nout))


def _proj_in_kernel(x_ref, mod_ref, g_ref, w_ref,
                    qa_ref, mkv_ref, qb_ref, nkv_ref, win_ref, ga_ref, gb_ref, gn_ref):
    x = x_ref[...]
    G, L, D = x.shape
    mod = mod_ref[...]
    u = _rms(x, g_ref[...]) * (1.0 + mod[:, 1:2, :]) + mod[:, 0:1, :]
    u = u.reshape(G * L, D).astype(BF16)

    def seg(a, b):
        return _dot(u, w_ref[:, a:b])

    qa_ref[...] = seg(_C_QA, _C_MKV)
    mkv_ref[...] = seg(_C_MKV, _C_QB)
    qb_ref[...] = seg(_C_QB, _C_NKV)
    nkv_ref[...] = seg(_C_NKV, _C_WIN)
    win_ref[...] = seg(_C_WIN, _C_GA)
    ga_ref[...] = _sigmoid(seg(_C_GA, _C_GB))
    gb_ref[...] = _sigmoid(seg(_C_GB, _C_GN))
    gn_ref[...] = _sigmoid(seg(_C_GN, _C_END))


def _proj_in(x3, mod3, g_attn, w_in_r, groups_per_mod, G):
    NG, L, D = x3.shape
    M = NG * L
    tm = G * L
    widths = (512, 1024, 512, 512, 256, 1024, 1024, GN_PAD)
    return pl.pallas_call(
        _proj_in_kernel,
        out_shape=[jax.ShapeDtypeStruct((M, w), F32) for w in widths],
        grid=(NG // G,),
        in_specs=[pl.BlockSpec((G, L, D), lambda i: (i, 0, 0)),
                  pl.BlockSpec((G, N_ADA, D), lambda i: (i // groups_per_mod, 0, 0)),
                  _const_spec((1, D)),
                  _const_spec(w_in_r.shape)],
        out_specs=[pl.BlockSpec((tm, w), lambda i: (i, 0)) for w in widths],
        compiler_params=_params(("parallel",)),
        name="proj_in",
    )(x3, mod3, g_attn.reshape(1, D), w_in_r)


def _out_kernel(x_ref, mod_ref, oa_ref, ob_ref, ga_ref, gb_ref, wpa_ref, wpb_ref, wout_ref,
                gffn_ref, wgu_ref, wdown_ref, gfin_ref, y_ref):
    x = x_ref[...]
    G, L, D = x.shape
    mod = mod_ref[...]
    ya = _dot(oa_ref[...].astype(BF16), wpa_ref[...])
    yb = _dot(ob_ref[...].astype(BF16), wpb_ref[...])
    mixed = _dot((ga_ref[...] * ya + gb_ref[...] * yb).astype(BF16), wout_ref[...])
    x1 = x + mod[:, 2:3, :] * mixed.reshape(G, L, D)
    h = _rms(x1, gffn_ref[...]) * (1.0 + mod[:, 4:5, :]) + mod[:, 3:4, :]
    ab = _dot(h.reshape(G * L, D).astype(BF16), wgu_ref[...])
    a = ab[:, :D_FF]
    b = ab[:, D_FF:]
    f = _dot((a * _sigmoid(a) * b).astype(BF16), wdown_ref[...])
    x2 = x1 + mod[:, 5:6, :] * f.reshape(G, L, D)
    y_ref[...] = _rms(x2, gfin_ref[...])


def _out_stage(x3, mod3, o_a, o_b, g_a, g_b, w, groups_per_mod, G):
    NG, L, D = x3.shape
    tm = G * L
    tok = lambda wd: pl.BlockSpec((tm, wd), lambda i: (i, 0))
    return pl.pallas_call(
        _out_kernel,
        out_shape=jax.ShapeDtypeStruct((NG, L, D), F32),
        grid=(NG // G,),
        in_specs=[pl.BlockSpec((G, L, D), lambda i: (i, 0, 0)),
                  pl.BlockSpec((G, N_ADA, D), lambda i: (i // groups_per_mod, 0, 0)),
                  tok(W_A), tok(W_B), tok(D), tok(D),
                  _const_spec(w['w_pa'].shape), _const_spec(w['w_pb'].shape), _const_spec(w['w_out'].shape),
                  _const_spec((1, D)), _const_spec(w['w_gu'].shape), _const_spec(w['w_down'].shape),
                  _const_spec((1, D))],
        out_specs=pl.BlockSpec((G, L, D), lambda i: (i, 0, 0)),
        compiler_params=_params(("parallel",)),
        name="out_stage",
    )(x3, mod3, o_a, o_b, g_a, g_b, w['w_pa'], w['w_pb'], w['w_out'],
      w['g_ffn'].reshape(1, D), w['w_gu'], w['w_down'], w['g_final'].reshape(1, D))


def _moba_prompt_kernel(slopes_ref, q_ref, k_ref, v_ref, o_ref):
    hp = pl.program_id(1)
    i = pl.program_id(2)
    tq = q_ref.shape[0]
    T = k_ref.shape[0]
    nb = T // MOBA_BLOCK
    qpos = i * tq + lax.broadcasted_iota(jnp.int32, (tq, 1), 0)
    kpos = lax.broadcasted_iota(jnp.int32, (1, T), 1)
    dist = (qpos - kpos).astype(F32)
    causal = kpos <= qpos
    lane = lax.broadcasted_iota(jnp.int32, (1, LANES), 1)
    lane_f = lane.astype(F32)
    expand = jnp.where(lax.broadcasted_iota(jnp.int32, (LANES, T), 1) // MOBA_BLOCK
                       == lax.broadcasted_iota(jnp.int32, (LANES, T), 0), 1.0, 0.0).astype(BF16)
    outs = []
    for hh in range(2):
        slope = slopes_ref[hp * 2 + hh]
        q = q_ref[:, hh * HEAD_DIM:(hh + 1) * HEAD_DIM]
        k = k_ref[:, hh * HEAD_DIM:(hh + 1) * HEAD_DIM]
        v = v_ref[:, hh * HEAD_DIM:(hh + 1) * HEAD_DIM]
        kmean = jnp.concatenate(
            [jnp.mean(k[j * MOBA_BLOCK:(j + 1) * MOBA_BLOCK, :], axis=0, keepdims=True) for j in range(nb)]
            + [jnp.zeros((LANES - nb, HEAD_DIM), F32)], axis=0)
        gate = _dot_nt_hilo(q, kmean)
        gate = jnp.where(lane < i, gate, jnp.where(lane < nb, NEG_INF, REMOVED))
        sel = _topk_pick(gate, MOBA_TOPK, lane_f)
        sel = jnp.where(lane < i, sel, jnp.where(lane == i, 1.0, 0.0))
        allowed = _dot(sel.astype(BF16), expand) > 0.5
        s = _dot_nt(q.astype(BF16), k.astype(BF16)) * SCALE - slope * dist
        e, l = _softmax_parts(s, allowed & causal)
        outs.append(_dot(e.astype(BF16), v.astype(BF16)) / l)
    o_ref[...] = jnp.concatenate(outs, axis=-1)


def _moba_prompt(q_a, moba_kv, slopes, B, T):
    tq = MOBA_BLOCK
    nq = T // tq
    hp_n = H_A // 2
    grid_spec = pltpu.PrefetchScalarGridSpec(
        num_scalar_prefetch=0,
        grid=(B, hp_n, nq),
        in_specs=[pl.BlockSpec(memory_space=pltpu.SMEM),
                  pl.BlockSpec((tq, LANES), lambda b, h, i: (b * nq + i, h)),
                  pl.BlockSpec((T, LANES), lambda b, h, i: (b, h)),
                  pl.BlockSpec((T, LANES), lambda b, h, i: (b, hp_n + h))],
        out_specs=pl.BlockSpec((tq, LANES), lambda b, h, i: (b * nq + i, h)),
    )
    return pl.pallas_call(
        _moba_prompt_kernel,
        out_shape=jax.ShapeDtypeStruct((B * T, W_A), F32),
        grid_spec=grid_spec,
        compiler_params=_params(("parallel", "parallel", "arbitrary")),
        name="moba_prompt",
    )(slopes, q_a, moba_kv, moba_kv)


def _compress_tokens(read_rows, nseg, c, w1_ref, pe_ref, b1_ref, w2_ref, b2_ref):
    acc = [jnp.zeros((nseg, 2 * CMP_HID), F32) for _ in range(KV_B)]
    cpe = jnp.zeros((1, CMP_HID), F32)
    for pos in range(CMP_STRIDE):
        rows = read_rows(pos).astype(BF16)
        wcat = w1_ref[c, pos]
        for kv in range(KV_B):
            acc[kv] = acc[kv] + _dot(rows[:, kv * HEAD_DIM:(kv + 1) * HEAD_DIM], wcat)
        pe2 = jnp.concatenate([pe_ref[c, pos:pos + 1, :], pe_ref[c, CMP_STRIDE + pos:CMP_STRIDE + pos + 1, :],
                               jnp.zeros((6, HEAD_DIM), F32)], axis=0).astype(BF16)
        r = _dot(pe2, wcat)
        cpe = cpe + r[0:1, :CMP_HID] + r[1:2, CMP_HID:]
    out = []
    for kv in range(KV_B):
        top = acc[kv][:, :CMP_HID]
        bot = acc[kv][:, CMP_HID:]
        bot = jnp.concatenate([bot[1:, :], bot[:1, :]], axis=0)
        hid = _gelu_tanh(top + bot + cpe + b1_ref[c])
        out.append(_dot(hid.astype(BF16), w2_ref[c]) + b2_ref[c])
    return out


def _compress_prompt_kernel(k_ref, v_ref, w1_ref, pe_ref, b1_ref, w2_ref, b2_ref, o_ref):
    T = k_ref.shape[0]
    nseg = T // CMP_STRIDE
    res = []
    for c, ref in enumerate((k_ref, v_ref)):
        read = lambda pos, ref=ref: ref[pl.ds(pos, nseg, stride=CMP_STRIDE), :]
        res += _compress_tokens(read, nseg, c, w1_ref, pe_ref, b1_ref, w2_ref, b2_ref)
    o_ref[0] = jnp.concatenate(res, axis=-1)


def _compress_prompt(nsa_kv, cw, B, T):
    nseg = T // CMP_STRIDE
    return pl.pallas_call(
        _compress_prompt_kernel,
        out_shape=jax.ShapeDtypeStruct((B, nseg, 4 * HEAD_DIM), F32),
        grid=(B,),
        in_specs=[pl.BlockSpec((T, W_KVB), lambda b: (b, 0)),
                  pl.BlockSpec((T, W_KVB), lambda b: (b, 1)),
                  _const_spec(cw['w1'].shape), _const_spec(cw['pe'].shape), _const_spec(cw['b1'].shape),
                  _const_spec(cw['w2'].shape), _const_spec(cw['b2'].shape)],
        out_specs=pl.BlockSpec((1, nseg, 4 * HEAD_DIM), lambda b: (b, 0, 0)),
        compiler_params=_params(("parallel",)),
        name="compress_prompt",
    )(nsa_kv, nsa_kv, cw['w1'], cw['pe'], cw['b1'], cw['w2'], cw['b2'])


def _sel_scores(psum, cur, n_cmp_lanes, nsb_lanes, nsb):
    cn = lax.broadcasted_iota(jnp.int32, (n_cmp_lanes, nsb_lanes), 0) * CMP_STRIDE
    sj = lax.broadcasted_iota(jnp.int32, (n_cmp_lanes, nsb_lanes), 1) * SEL_BLOCK
    overlap = jnp.where((cn < sj + SEL_BLOCK) & (cn + CMP_LEN > sj), 1.0, 0.0).astype(BF16)
    hi, lo = _split(psum)
    imp = _dot(hi, overlap) + _dot(lo, overlap)
    jr = lax.broadcasted_iota(jnp.int32, (1, nsb_lanes), 1)
    forced = (jr == 0) | ((jr <= cur) & (jr > cur - SEL_LOCAL))
    imp = jnp.where(jr <= cur, jnp.where(forced, FORCE_SCORE, imp), jnp.where(jr < nsb, NEG_INF, REMOVED))
    sel = _topk_pick(imp, min(SEL_TOPN, nsb), jr.astype(F32))
    return jnp.where(jr <= cur, sel, 0.0)


def _nsa_prompt_kernel(slopes_ref, q_ref, ckv_ref, kv_ref, win_ref, gn_ref, o_ref):
    i = pl.program_id(1)
    tq = q_ref.shape[0]
    T = kv_ref.shape[0]
    n_cmp = ckv_ref.shape[1]
    nsb = T // SEL_BLOCK
    wlen = min(WINDOW + tq, T)
    qpos = i * tq + lax.broadcasted_iota(jnp.int32, (tq, 1), 0)
    kpos = lax.broadcasted_iota(jnp.int32, (1, T), 1)
    dist = (qpos - kpos).astype(F32)
    causal = kpos <= qpos
    c_end = lax.broadcasted_iota(jnp.int32, (1, n_cmp), 1) * CMP_STRIDE + (CMP_LEN - 1)
    cmp_mask = (c_end <= qpos) & (c_end < T)
    cur = qpos // SEL_BLOCK
    expand = jnp.where(lax.broadcasted_iota(jnp.int32, (nsb, T), 1) // SEL_BLOCK
                       == lax.broadcasted_iota(jnp.int32, (nsb, T), 0), 1.0, 0.0).astype(BF16)
    w0 = pl.multiple_of(jnp.maximum(i * tq + tq - wlen, 0), 8)
    wpos = w0 + lax.broadcasted_iota(jnp.int32, (1, wlen), 1)
    wdist = (qpos - wpos).astype(F32)
    wmask = (wpos <= qpos) & (wpos > qpos - WINDOW)
    gn = gn_ref[...]
    outs = []
    for kv in range(KV_B):
        lo, hi = kv * HEAD_DIM, (kv + 1) * HEAD_DIM
        ck = ckv_ref[0, :, lo:hi].astype(BF16)
        cv = ckv_ref[0, :, W_KVB + lo:W_KVB + hi].astype(BF16)
        ks = kv_ref[:, lo:hi].astype(BF16)
        vs = kv_ref[:, W_KVB + lo:W_KVB + hi].astype(BF16)
        kw = win_ref[pl.ds(w0, wlen), lo:hi].astype(BF16)
        vw = win_ref[pl.ds(w0, wlen), W_KVB + lo:W_KVB + hi].astype(BF16)
        qs, o_cmp = [], []
        psum = jnp.zeros((tq, n_cmp), F32)
        for g in range(GROUP_B):
            h = kv * GROUP_B + g
            q = q_ref[:, h * HEAD_DIM:(h + 1) * HEAD_DIM].astype(BF16)
            qs.append(q)
            e, l = _softmax_parts(_dot_nt(q, ck) * SCALE, cmp_mask)
            p = e / l
            psum = psum + p
            o_cmp.append(_dot(p.astype(BF16), cv))
        sel = _sel_scores(psum, cur, n_cmp, nsb, nsb)
        allowed = (_dot(sel.astype(BF16), expand) > 0.5) & causal
        for g in range(GROUP_B):
            h = kv * GROUP_B + g
            slope = slopes_ref[h]
            e, l = _softmax_parts(_dot_nt(qs[g], ks) * SCALE - slope * dist, allowed)
            o_sel = _dot(e.astype(BF16), vs) / l
            e, l = _softmax_parts(_dot_nt(qs[g], kw) * SCALE - slope * wdist, wmask)
            o_win = _dot(e.astype(BF16), vw) / l
            outs.append(gn[:, 3 * h:3 * h + 1] * o_cmp[g] + gn[:, 3 * h + 1:3 * h + 2] * o_sel
                        + gn[:, 3 * h + 2:3 * h + 3] * o_win)
    o_ref[...] = jnp.concatenate(outs, axis=-1)


def _nsa_prompt(q_b, ckv, nsa_kv, win, gn, slopes, B, T):
    tq = 256
    nq = T // tq
    grid_spec = pltpu.PrefetchScalarGridSpec(
        num_scalar_prefetch=0,
        grid=(B, nq),
        in_specs=[pl.BlockSpec(memory_space=pltpu.SMEM),
                  pl.BlockSpec((tq, W_B), lambda b, i: (b * nq + i, 0)),
                  pl.BlockSpec((1,) + ckv.shape[1:], lambda b, i: (b, 0, 0)),
                  pl.BlockSpec((T, 2 * W_KVB), lambda b, i: (b, 1)),
                  pl.BlockSpec((T, 2 * W_KVB), lambda b, i: (b, 0)),
                  pl.BlockSpec((tq, GN_PAD), lambda b, i: (b * nq + i, 0))],
        out_specs=pl.BlockSpec((tq, W_B), lambda b, i: (b * nq + i, 0)),
    )
    return pl.pallas_call(
        _nsa_prompt_kernel,
        out_shape=jax.ShapeDtypeStruct((B * T, W_B), F32),
        grid_spec=grid_spec,
        compiler_params=_params(("parallel", "arbitrary")),
        name="nsa_prompt",
    )(slopes, q_b, ckv, nsa_kv, win, gn)


def _block_diag_q(q, rows_per_head, n_heads):
    t, w = q.shape
    qt = jnp.concatenate([q] * n_heads, axis=0)
    r = lax.broadcasted_iota(jnp.int32, (n_heads * t, w), 0) // rows_per_head
    c = lax.broadcasted_iota(jnp.int32, (n_heads * t, w), 1) // HEAD_DIM
    return jnp.where(r == c, qt, 0.0)


def _diag_heads(o, rows_per_head, n_heads):
    return jnp.concatenate(
        [o[h * rows_per_head:(h + 1) * rows_per_head, h * HEAD_DIM:(h + 1) * HEAD_DIM] for h in range(n_heads)],
        axis=0)


def _moba_sample_kernel(pt_ref, slopes_ref, q_ref, new_ref, p0_ref, p1_ref, o_ref,
                        qbd_ref, m_ref, l_ref, g_ref, acc_ref, *, past):
    j = pl.program_id(1)
    nblk = pl.num_programs(1)
    t = q_ref.shape[0]
    R = H_A * t
    row_h = lax.broadcasted_iota(jnp.int32, (R, 1), 0) // t
    row_t = lax.broadcasted_iota(jnp.int32, (R, 1), 0) % t
    slope = jnp.zeros((R, 1), F32)
    for h in range(H_A):
        slope = jnp.where(row_h == h, slopes_ref[h], slope)
    qpos = past + row_t
    lane = lax.broadcasted_iota(jnp.int32, (1, LANES), 1)

    @pl.when(j == 0)
    def _():
        qbd_ref[...] = _block_diag_q(q_ref[...], t, H_A)
        m_ref[...] = jnp.full(m_ref.shape, NEG_INF, F32)
        l_ref[...] = jnp.zeros(l_ref.shape, F32)
        g_ref[...] = jnp.full(g_ref.shape, REMOVED, F32)

    qbd = qbd_ref[...]
    k = jnp.concatenate([p0_ref[0, :, :W_A], p1_ref[0, :, :W_A]], axis=0)
    v = jnp.concatenate([p0_ref[0, :, W_A:], p1_ref[0, :, W_A:]], axis=0)
    kmean = jnp.mean(k, axis=0, keepdims=True)
    gate = jnp.sum(qbd * kmean, axis=-1, keepdims=True)
    kpos = j * MOBA_BLOCK + lax.broadcasted_iota(jnp.int32, (1, MOBA_BLOCK), 1)
    s = _dot_nt(qbd.astype(BF16), k.astype(BF16)) * SCALE - slope * (qpos - kpos).astype(F32)
    m = jnp.max(s, axis=-1, keepdims=True)
    e = jnp.exp(s - m)
    l = jnp.sum(e, axis=-1, keepdims=True)
    acc_ref[j] = _diag_heads(_dot(e.astype(BF16), v.astype(BF16)), t, H_A)
    m_ref[...] = jnp.where(lane == j, m, m_ref[...])
    l_ref[...] = jnp.where(lane == j, l, l_ref[...])
    g_ref[...] = jnp.where(lane == j, gate, g_ref[...])

    @pl.when(j == nblk - 1)
    def _():
        sel = _topk_pick(g_ref[...], MOBA_TOPK, lane.astype(F32))
        sel = jnp.where(lane < nblk, sel, 0.0) > 0.5
        kn = new_ref[:, :W_A]
        vn = new_ref[:, W_A:]
        npos = past + lax.broadcasted_iota(jnp.int32, (1, t), 1)
        s_own = _dot_nt(qbd.astype(BF16), kn.astype(BF16)) * SCALE - slope * (qpos - npos).astype(F32)
        own_mask = npos <= qpos
        s_own = jnp.where(own_mask, s_own, NEG_INF)
        m_own = jnp.max(s_own, axis=-1, keepdims=True)
        m_all = m_ref[...]
        m_tot = jnp.maximum(jnp.max(jnp.where(sel, m_all, NEG_INF), axis=-1, keepdims=True), m_own)
        e_own = jnp.where(own_mask, jnp.exp(s_own - m_tot), 0.0)
        wgt = jnp.where(sel, jnp.exp(m_all - m_tot), 0.0)
        l_tot = jnp.sum(wgt * l_ref[...], axis=-1, keepdims=True) + jnp.sum(e_own, axis=-1, keepdims=True)
        o = _diag_heads(_dot(e_own.astype(BF16), vn.astype(BF16)), t, H_A)
        for b in range(acc_ref.shape[0]):
            o = o + wgt[:, b:b + 1] * acc_ref[b]
        o = o / jnp.maximum(l_tot, 1e-30)
        for h in range(H_A):
            o_ref[:, h * HEAD_DIM:(h + 1) * HEAD_DIM] = o[h * t:(h + 1) * t, :]


def _moba_sample(q_a, moba_new, cache, page_table, slopes, DB, t):
    n_pool, page, width = cache.shape
    n_pages = page_table.shape[1]
    past = n_pages * page
    assert page * 2 == MOBA_BLOCK and past % MOBA_BLOCK == 0 and t <= MOBA_BLOCK
    nblk = n_pages // 2
    assert nblk <= LANES
    R = H_A * t
    grid_spec = pltpu.PrefetchScalarGridSpec(
        num_scalar_prefetch=1,
        grid=(DB, nblk),
        in_specs=[pl.BlockSpec(memory_space=pltpu.SMEM),
                  pl.BlockSpec((t, W_A), lambda b, j, pt: (b, 0)),
                  pl.BlockSpec((t, 2 * W_A), lambda b, j, pt: (b, 0)),
                  pl.BlockSpec((1, page, width), lambda b, j, pt: (pt[b, 2 * j], 0, 0)),
                  pl.BlockSpec((1, page, width), lambda b, j, pt: (pt[b, 2 * j + 1], 0, 0))],
        out_specs=pl.BlockSpec((t, W_A), lambda b, j, pt: (b, 0)),
        scratch_shapes=[pltpu.VMEM((R, W_A), F32),
                        pltpu.VMEM((R, LANES), F32), pltpu.VMEM((R, LANES), F32), pltpu.VMEM((R, LANES), F32),
                        pltpu.VMEM((nblk, R, HEAD_DIM), F32)],
    )
    return pl.pallas_call(
        functools.partial(_moba_sample_kernel, past=past),
        out_shape=jax.ShapeDtypeStruct((DB * t, W_A), F32),
        grid_spec=grid_spec,
        compiler_params=_params(("parallel", "arbitrary")),
        name="moba_sample",
    )(page_table, slopes, q_a, moba_new, cache, cache)


def _nsa_sample_kernel(pt_ref, slopes_ref, q_ref, new_ref, wnew_ref, gn_ref, wcache_ref, p0_ref, p1_ref,
                       w1_ref, pe_ref, b1_ref, w2_ref, b2_ref,
                       o_ref, wout_ref, cmpk_ref, cmpv_ref, sel_ref, *, past):
    j = pl.program_id(1)
    nstep = pl.num_programs(1)
    page = p0_ref.shape[1]
    t = q_ref.shape[0]
    L = past + t
    nseg = cmpk_ref.shape[0] // CMP_STRIDE
    n_cmp = -(-L // CMP_STRIDE) - 1
    nsb = -(-L // SEL_BLOCK)
    nsb_lanes = -(-nsb // LANES) * LANES
    lsel = sel_ref.shape[0]

    for r, pref in enumerate((p0_ref, p1_ref)):
        base = pl.multiple_of((2 * j + r) * page, page)
        cmpk_ref[pl.ds(base, page), :] = pref[0, :, :W_KVB]
        cmpv_ref[pl.ds(base, page), :] = pref[0, :, W_KVB:2 * W_KVB]
        sel_ref[pl.ds(base, page), :] = pref[0, :, 2 * W_KVB:].astype(BF16)

    @pl.when(j == nstep - 1)
    def _():
        tail = cmpk_ref.shape[0] - past
        pad = jnp.zeros((tail - t, W_KVB), F32)
        cmpk_ref[pl.ds(past, tail), :] = jnp.concatenate([new_ref[:, :W_KVB], pad], axis=0)
        cmpv_ref[pl.ds(past, tail), :] = jnp.concatenate([new_ref[:, W_KVB:2 * W_KVB], pad], axis=0)
        sel_ref[pl.ds(past, lsel - past), :] = jnp.concatenate(
            [new_ref[:, 2 * W_KVB:], jnp.zeros((lsel - past - t, 2 * W_KVB), F32)], axis=0).astype(BF16)

        toks = []
        for c, ref in enumerate((cmpk_ref, cmpv_ref)):
            read = lambda pos, ref=ref: ref[pl.ds(pos, nseg, stride=CMP_STRIDE), :]
            toks += _compress_tokens(read, nseg, c, w1_ref, pe_ref, b1_ref, w2_ref, b2_ref)

        R = GROUP_B * t
        row_t = lax.broadcasted_iota(jnp.int32, (R, 1), 0) % t
        row_g = lax.broadcasted_iota(jnp.int32, (R, 1), 0) // t
        qpos = past + row_t
        c_end = lax.broadcasted_iota(jnp.int32, (1, nseg), 1) * CMP_STRIDE + (CMP_LEN - 1)
        cmp_mask = (c_end <= qpos) & (lax.broadcasted_iota(jnp.int32, (1, nseg), 1) < n_cmp)
        kpos = lax.broadcasted_iota(jnp.int32, (1, lsel), 1)
        dist = (qpos - kpos).astype(F32)
        causal = (kpos <= qpos) & (kpos < L)
        chunk = LANES * SEL_BLOCK
        expand = jnp.where(lax.broadcasted_iota(jnp.int32, (LANES, chunk), 1) // SEL_BLOCK
                           == lax.broadcasted_iota(jnp.int32, (LANES, chunk), 0), 1.0, 0.0).astype(BF16)
        n_buf = wcache_ref.shape[1]
        wk = jnp.concatenate([wcache_ref[0], wnew_ref[...]], axis=0)
        wpos = past - n_buf + lax.broadcasted_iota(jnp.int32, (1, n_buf + t), 1)
        wdist = (qpos - wpos).astype(F32)
        wmask = (wpos <= qpos) & (wpos > qpos - WINDOW) & (wpos >= 0)
        wout_ref[0] = wk[t:, :]
        gn = gn_ref[...]
        for kv in range(KV_B):
            lo, hi = kv * HEAD_DIM, (kv + 1) * HEAD_DIM
            q = jnp.concatenate([q_ref[:, (kv * GROUP_B + g) * HEAD_DIM:(kv * GROUP_B + g + 1) * HEAD_DIM]
                                 for g in range(GROUP_B)], axis=0).astype(BF16)
            slope = jnp.zeros((R, 1), F32)
            for g in range(GROUP_B):
                slope = jnp.where(row_g == g, slopes_ref[kv * GROUP_B + g], slope)
            ck = toks[kv].astype(BF16)
            cv = toks[KV_B + kv].astype(BF16)
            e, l = _softmax_parts(_dot_nt(q, ck) * SCALE, cmp_mask)
            p = e / l
            o_cmp = _dot(p.astype(BF16), cv)
            psum = p[0:t]
            for g in range(1, GROUP_B):
                psum = psum + p[g * t:(g + 1) * t]
            sel = _sel_scores(psum, qpos[0:t] // SEL_BLOCK, nseg, nsb_lanes, nsb)
            sel = jnp.concatenate([sel] * GROUP_B, axis=0)
            selb = sel.astype(BF16)
            allowed = jnp.concatenate(
                [_dot(selb[:, ci * LANES:(ci + 1) * LANES], expand[:, :min(chunk, lsel - ci * chunk)])
                 for ci in range(nsb_lanes // LANES)], axis=-1)
            allowed = (allowed > 0.5) & causal
            ks = sel_ref[:, lo:hi]
            vs = sel_ref[:, W_KVB + lo:W_KVB + hi]
            e, l = _softmax_parts(_dot_nt(q, ks) * SCALE - slope * dist, allowed)
            o_sel = _dot(e.astype(BF16), vs) / l
            e, l = _softmax_parts(_dot_nt(q, wk[:, lo:hi].astype(BF16)) * SCALE - slope * wdist, wmask)
            o_win = _dot(e.astype(BF16), wk[:, W_KVB + lo:W_KVB + hi].astype(BF16)) / l
            for g in range(GROUP_B):
                h = kv * GROUP_B + g
                rs = slice(g * t, (g + 1) * t)
                o_ref[:, h * HEAD_DIM:(h + 1) * HEAD_DIM] = (
                    gn[:, 3 * h:3 * h + 1] * o_cmp[rs] + gn[:, 3 * h + 1:3 * h + 2] * o_sel[rs]
                    + gn[:, 3 * h + 2:3 * h + 3] * o_win[rs])


def _nsa_sample(q_b, nsa_new, win_new, gn, win_cache, cache, page_table, cw, slopes, DB, t):
    n_pool, page, width = cache.shape
    n_pages = page_table.shape[1]
    past = n_pages * page
    n_buf = win_cache.shape[1]
    assert past % SEL_BLOCK == 0 and past % CMP_STRIDE == 0 and n_pages % 2 == 0 and t <= CMP_STRIDE
    nseg = past // CMP_STRIDE + 1
    nseg_pad = -(-(nseg) // 8) * 8
    lsel = past + SEL_BLOCK
    grid_spec = pltpu.PrefetchScalarGridSpec(
        num_scalar_prefetch=1,
        grid=(DB, n_pages // 2),
        in_specs=[pl.BlockSpec(memory_space=pltpu.SMEM),
                  pl.BlockSpec((t, W_B), lambda b, j, pt: (b, 0)),
                  pl.BlockSpec((t, 4 * W_KVB), lambda b, j, pt: (b, 0)),
                  pl.BlockSpec((t, 2 * W_KVB), lambda b, j, pt: (b, 0)),
                  pl.BlockSpec((t, GN_PAD), lambda b, j, pt: (b, 0)),
                  pl.BlockSpec((1, n_buf, 2 * W_KVB), lambda b, j, pt: (b, 0, 0)),
                  pl.BlockSpec((1, page, width), lambda b, j, pt: (pt[b, 2 * j], 0, 0)),
                  pl.BlockSpec((1, page, width), lambda b, j, pt: (pt[b, 2 * j + 1], 0, 0))]
                 + [pl.BlockSpec(cw[n].shape, lambda b, j, pt, nd=cw[n].ndim: (0,) * nd)
                    for n in ('w1', 'pe', 'b1', 'w2', 'b2')],
        out_specs=[pl.BlockSpec((t, W_B), lambda b, j, pt: (b, 0)),
                   pl.BlockSpec((1, n_buf, 2 * W_KVB), lambda b, j, pt: (b, 0, 0))],
        scratch_shapes=[pltpu.VMEM((nseg_pad * CMP_STRIDE, W_KVB), F32),
                        pltpu.VMEM((nseg_pad * CMP_STRIDE, W_KVB), F32),
                        pltpu.VMEM((lsel, 2 * W_KVB), BF16)],
    )
    return pl.pallas_call(
        functools.partial(_nsa_sample_kernel, past=past),
        out_shape=[jax.ShapeDtypeStruct((DB * t, W_B), F32),
                   jax.ShapeDtypeStruct(win_cache.shape, F32)],
        grid_spec=grid_spec,
        compiler_params=_params(("parallel", "arbitrary")),
        name="nsa_sample",
    )(page_table, slopes, q_b, nsa_new, win_new, gn, win_cache, cache, cache,
      cw['w1'], cw['pe'], cw['b1'], cw['w2'], cw['b2'])


def _alibi_slopes(n):
    return jnp.exp2(-8.0 * (jnp.arange(n, dtype=F32) + 1.0) / n)


def _prep_weights(w_in, cmp_pe, cmp_w1, cmp_b1, cmp_w2, cmp_b2):
    gn = w_in[:, _C_GA:_C_GA + 3 * H_B]
    w_in_r = jnp.concatenate(
        [w_in[:, :_C_GA], w_in[:, _C_GA + 3 * H_B:],
         gn, jnp.zeros((D_MODEL, GN_PAD - 3 * H_B), w_in.dtype)], axis=1).astype(BF16)
    half = CMP_STRIDE * HEAD_DIM
    w1 = jnp.concatenate([cmp_w1[:, :half].reshape(2, CMP_STRIDE, HEAD_DIM, CMP_HID),
                          cmp_w1[:, half:].reshape(2, CMP_STRIDE, HEAD_DIM, CMP_HID)], axis=-1).astype(BF16)
    cw = {'w1': w1, 'pe': cmp_pe, 'b1': cmp_b1.reshape(2, 1, CMP_HID),
          'w2': cmp_w2.astype(BF16), 'b2': cmp_b2.reshape(2, 1, HEAD_DIM)}
    return w_in_r, cw


def kernel(x_prompt, x_sample, cache_moba_kv, cache_nsa_kv, cache_nsa_win, page_table, c_prompt, c_sample,
           w_ada, b_ada, g_attn, w_in, cmp_pe, cmp_w1, cmp_b1, cmp_w2, cmp_b2, w_pa, w_pb, w_out,
           g_ffn, w_gu, w_down, g_final):
    B, T, D = x_prompt.shape
    DB, t, _ = x_sample.shape
    n_pool, page = cache_moba_kv.shape[:2]
    n_buf = cache_nsa_win.shape[1]
    assert T % MOBA_BLOCK == 0 and T >= WINDOW

    w_in_r, cw = _prep_weights(w_in, cmp_pe, cmp_w1, cmp_b1, cmp_w2, cmp_b2)
    wo = {'w_pa': w_pa.astype(BF16), 'w_pb': w_pb.astype(BF16), 'w_out': w_out.astype(BF16),
          'w_gu': w_gu.astype(BF16), 'w_down': w_down.astype(BF16), 'g_ffn': g_ffn, 'g_final': g_final}
    slopes_a = _alibi_slopes(H_A)
    slopes_b = _alibi_slopes(H_B)

    mod = _ada(jnp.concatenate([c_prompt, c_sample], axis=0), w_ada.astype(BF16), b_ada)
    mod = mod.reshape(B + DB, N_ADA, D)
    mod_p, mod_s = mod[:B], mod[B:]

    tm = 512
    xp3 = x_prompt.reshape(B * T // tm, tm, D)
    qa, mkv, qb, nkv, win, ga, gb, gn = _proj_in(xp3, mod_p, g_attn, w_in_r, T // tm, 1)
    o_a = _moba_prompt(qa, mkv, slopes_a, B, T)
    ckv = _compress_prompt(nkv, cw, B, T)
    o_b = _nsa_prompt(qb, ckv, nkv, win, gn, slopes_b, B, T)
    tm_o = 256
    y_prompt = _out_stage(x_prompt.reshape(B * T // tm_o, tm_o, D), mod_p, o_a, o_b, ga, gb, wo,
                          T // tm_o, 1).reshape(B, T, D)
    moba_kv_prompt = mkv.reshape(B, T, 2, H_A, HEAD_DIM)
    nsa_kv_prompt = nkv.reshape(B, T, 4, KV_B, HEAD_DIM)
    keep = min(WINDOW, T)
    nsa_win_prompt = win.reshape(B, T, 2, KV_B, HEAD_DIM)[:, T - keep:]

    qa, mkv, qb, nkv, win, ga, gb, gn = _proj_in(x_sample, mod_s, g_attn, w_in_r, 1, DB)
    o_a = _moba_sample(qa, mkv, cache_moba_kv.reshape(n_pool, page, 2 * W_A), page_table, slopes_a, DB, t)
    o_b, win_s = _nsa_sample(qb, nkv, win, gn, cache_nsa_win.reshape(DB, n_buf, 2 * W_KVB),
                             cache_nsa_kv.reshape(n_pool, page, 4 * W_KVB), page_table, cw, slopes_b, DB, t)
    y_sample = _out_stage(x_sample, mod_s, o_a, o_b, ga, gb, wo, 1, DB)
    moba_kv_sample = mkv.reshape(DB, t, 2, H_A, HEAD_DIM)
    nsa_kv_sample = nkv.reshape(DB, t, 4, KV_B, HEAD_DIM)
    nsa_win_sample = win_s.reshape(DB, n_buf, 2, KV_B, HEAD_DIM)

    return (y_prompt, y_sample, moba_kv_prompt, nsa_kv_prompt, nsa_win_prompt,
            moba_kv_sample, nsa_kv_sample, nsa_win_sample)
```

```python
import functools

import numpy as np
import jax
import jax.numpy as jnp
from jax import lax
from jax.experimental import pallas as pl
from jax.experimental.pallas import tpu as pltpu

F32 = jnp.float32
BF16 = jnp.bfloat16

D_MODEL = 1024
HEAD_DIM = 64
H_A = 8
W_A = H_A * HEAD_DIM
MOBA_BLOCK = 256
MOBA_TOPK = 3
H_B = 8
KV_B = 2
GROUP_B = H_B // KV_B
W_B = H_B * HEAD_DIM
W_KVB = KV_B * HEAD_DIM
CMP_LEN = 32
CMP_STRIDE = 16
CMP_HID = 256
SEL_BLOCK = 64
SEL_TOPN = 16
SEL_LOCAL = 2
WINDOW = 512
D_FF = -(-8 * D_MODEL // (3 * 256)) * 256
N_ADA = 6
EPS = 1e-6
NEG_INF = -1e30
FORCE_SCORE = 1e9
REMOVED = -3.0e38
SCALE = HEAD_DIM ** -0.5
LANES = 128
GN_PAD = LANES
VMEM_LIMIT = 56 * 1024 * 1024
PAGES_PER_STEP = 8
TQ = 256

_C_QA, _C_MKV, _C_QB, _C_NKV, _C_WIN, _C_GA, _C_GB, _C_GN, _C_END = (
    0, 512, 1536, 2048, 2560, 2816, 3840, 4864, 4992)


def _sigmoid(x):
    return 1.0 / (1.0 + jnp.exp(-x))


def _rms(x, g):
    return x * lax.rsqrt(jnp.mean(x * x, axis=-1, keepdims=True) + EPS) * g


def _dot(a, b):
    return jnp.dot(a, b, preferred_element_type=F32)


def _dot_nt(a, b):
    return lax.dot_general(a, b, (((1,), (1,)), ((), ())), preferred_element_type=F32)


def _split(x):
    hi = x.astype(BF16)
    lo = (x - hi.astype(F32)).astype(BF16)
    return hi, lo


def _dot_hilo(a, b):
    ah, al = _split(a)
    bh, bl = _split(b)
    return _dot(ah, bh) + (_dot(ah, bl) + _dot(al, bh))


def _softmax_parts(s, mask):
    s = jnp.where(mask, s, NEG_INF)
    m = jnp.max(s, axis=-1, keepdims=True)
    e = jnp.where(mask, jnp.exp(s - m), 0.0)
    l = jnp.maximum(jnp.sum(e, axis=-1, keepdims=True), 1e-30)
    return e, l


def _topk_pick(score, k, lane_f):
    big = float(score.shape[-1])
    picked = jnp.zeros(score.shape, F32)
    g = score
    for _ in range(k):
        mx = jnp.max(g, axis=-1, keepdims=True)
        idx = jnp.min(jnp.where(g == mx, lane_f, big), axis=-1, keepdims=True)
        pick = lane_f == idx
        picked = jnp.where(pick, 1.0, picked)
        g = jnp.where(pick, REMOVED, g)
    return picked


def _gelu_tanh(x):
    return 0.5 * x * (1.0 + jnp.tanh(np.sqrt(2.0 / np.pi) * (x + 0.044715 * (x * x * x))))


def _const_spec(shape):
    nd = len(shape)
    return pl.BlockSpec(shape, lambda *_: (0,) * nd, pipeline_mode=pl.Buffered(1))


def _params(sem):
    return pltpu.CompilerParams(dimension_semantics=sem, vmem_limit_bytes=VMEM_LIMIT)


def _block_flags_to_keys(n_blocks_lanes, n_keys, block):
    return jnp.where(lax.broadcasted_iota(jnp.int32, (n_blocks_lanes, n_keys), 1) // block
                     == lax.broadcasted_iota(jnp.int32, (n_blocks_lanes, n_keys), 0), 1.0, 0.0).astype(BF16)


def _ada_kernel(c_ref, w_ref, b_ref, o_ref):
    c = c_ref[...]
    s = (c * _sigmoid(c)).astype(BF16)
    o_ref[...] = _dot(s, w_ref[...]) + b_ref[...]


def _ada(c, w_bf, b):
    n, d = c.shape
    nout = w_bf.shape[1]
    tn = 1024
    return pl.pallas_call(
        _ada_kernel,
        out_shape=jax.ShapeDtypeStruct((n, nout), F32),
        grid=(nout // tn,),
        in_specs=[pl.BlockSpec((n, d), lambda j: (0, 0)),
                  pl.BlockSpec((d, tn), lambda j: (0, j)),
                  pl.BlockSpec((1, tn), lambda j: (0, j))],
        out_specs=pl.BlockSpec((n, tn), lambda j: (0, j)),
        compiler_params=_params(("arbitrary",)),
        name="ada_mod",
    )(c, w_bf, b.reshape(1, nout))


def _proj_in_kernel(x_ref, mod_ref, g_ref, w_ref, *refs, transposed):
    x = x_ref[...]
    G, L, D = x.shape
    mod = mod_ref[...]
    u = _rms(x, g_ref[...]) * (1.0 + mod[:, 1:2, :]) + mod[:, 0:1, :]
    u = u.reshape(G * L, D).astype(BF16)

    def seg(a, b):
        return _dot(u, w_ref[:, a:b])

    if transposed:
        wt_ref, qa_ref, qb_ref, ga_ref, gb_ref, gn_ref, kc_ref, vc_ref, mkv_ref, nkv_ref, win_ref = refs
        tm = G * L
        n_m, n_n, n_w = 2 * W_A, 4 * W_KVB, 2 * W_KVB
        mkv_ref[0] = _dot_nt(wt_ref[0:n_m, :], u).reshape(2, H_A, HEAD_DIM, tm)
        nkv_ref[0] = _dot_nt(wt_ref[n_m:n_m + n_n, :], u).reshape(4, KV_B, HEAD_DIM, tm)
        win_ref[0] = _dot_nt(wt_ref[n_m + n_n:n_m + n_n + n_w, :], u).reshape(2, KV_B, HEAD_DIM, tm)
        kc_ref[...] = seg(_C_NKV, _C_NKV + W_KVB)
        vc_ref[...] = seg(_C_NKV + W_KVB, _C_NKV + 2 * W_KVB)
    else:
        qa_ref, qb_ref, ga_ref, gb_ref, gn_ref, mkv_ref, nkv_ref, win_ref = refs
        mkv_ref[...] = seg(_C_MKV, _C_QB)
        nkv_ref[...] = seg(_C_NKV, _C_WIN)
        win_ref[...] = seg(_C_WIN, _C_GA)
    qa_ref[...] = seg(_C_QA, _C_MKV)
    qb_ref[...] = seg(_C_QB, _C_NKV)
    ga_ref[...] = _sigmoid(seg(_C_GA, _C_GB))
    gb_ref[...] = _sigmoid(seg(_C_GB, _C_GN))
    gn_ref[...] = _sigmoid(seg(_C_GN, _C_END))


def _proj_in(x3, mod3, g_attn, w_in_r, groups_per_mod, G, w_kvt=None, batch_len=None):
    NG, L, D = x3.shape
    M = NG * L
    tm = G * L
    tok = lambda w: (jax.ShapeDtypeStruct((M, w), F32), pl.BlockSpec((tm, w), lambda i: (i, 0)))
    outs = [tok(W_A), tok(W_B), tok(D), tok(D), tok(GN_PAD)]
    ins = [pl.BlockSpec((G, L, D), lambda i: (i, 0, 0)),
           pl.BlockSpec((G, N_ADA, D), lambda i: (i // groups_per_mod, 0, 0)),
           _const_spec((1, D)), _const_spec(w_in_r.shape)]
    args = [x3, mod3, g_attn.reshape(1, D), w_in_r]
    if w_kvt is None:
        outs += [tok(2 * W_A), tok(4 * W_KVB), tok(2 * W_KVB)]
    else:
        assert G == 1
        T = batch_len
        nt = T // tm
        tr = lambda s, h: (jax.ShapeDtypeStruct((M // T, s, h, HEAD_DIM, T), F32),
                           pl.BlockSpec((1, s, h, HEAD_DIM, tm), lambda i: (i // nt, 0, 0, 0, i % nt)))
        outs += [tok(W_KVB), tok(W_KVB), tr(2, H_A), tr(4, KV_B), tr(2, KV_B)]
        ins.append(_const_spec(w_kvt.shape))
        args.append(w_kvt)
    return pl.pallas_call(
        functools.partial(_proj_in_kernel, transposed=w_kvt is not None),
        out_shape=[o[0] for o in outs],
        grid=(NG // G,),
        in_specs=ins,
        out_specs=[o[1] for o in outs],
        compiler_params=_params(("parallel",)),
        name="proj_in",
    )(*args)


def _out_kernel(x_ref, mod_ref, oa_ref, ob_ref, ga_ref, gb_ref, wpa_ref, wpb_ref, wout_ref,
                gffn_ref, wgu_ref, wdown_ref, gfin_ref, y_ref):
    x = x_ref[...]
    G, L, D = x.shape
    mod = mod_ref[...]
    ya = _dot(oa_ref[...].astype(BF16), wpa_ref[...])
    yb = _dot(ob_ref[...].astype(BF16), wpb_ref[...])
    mixed = _dot((ga_ref[...] * ya + gb_ref[...] * yb).astype(BF16), wout_ref[...])
    x1 = x + mod[:, 2:3, :] * mixed.reshape(G, L, D)
    h = _rms(x1, gffn_ref[...]) * (1.0 + mod[:, 4:5, :]) + mod[:, 3:4, :]
    ab = _dot(h.reshape(G * L, D).astype(BF16), wgu_ref[...])
    a = ab[:, :D_FF]
    b = ab[:, D_FF:]
    f = _dot((a * _sigmoid(a) * b).astype(BF16), wdown_ref[...])
    x2 = x1 + mod[:, 5:6, :] * f.reshape(G, L, D)
    y_ref[...] = _rms(x2, gfin_ref[...])


def _out_stage(x3, mod3, o_a, o_b, g_a, g_b, w, groups_per_mod, G):
    NG, L, D = x3.shape
    tm = G * L
    tok = lambda wd: pl.BlockSpec((tm, wd), lambda i: (i, 0))
    return pl.pallas_call(
        _out_kernel,
        out_shape=jax.ShapeDtypeStruct((NG, L, D), F32),
        grid=(NG // G,),
        in_specs=[pl.BlockSpec((G, L, D), lambda i: (i, 0, 0)),
                  pl.BlockSpec((G, N_ADA, D), lambda i: (i // groups_per_mod, 0, 0)),
                  tok(W_A), tok(W_B), tok(D), tok(D),
                  _const_spec(w['w_pa'].shape), _const_spec(w['w_pb'].shape), _const_spec(w['w_out'].shape),
                  _const_spec((1, D)), _const_spec(w['w_gu'].shape), _const_spec(w['w_down'].shape),
                  _const_spec((1, D))],
        out_specs=pl.BlockSpec((G, L, D), lambda i: (i, 0, 0)),
        compiler_params=_params(("parallel",)),
        name="out_stage",
    )(x3, mod3, o_a, o_b, g_a, g_b, w['w_pa'], w['w_pb'], w['w_out'],
      w['g_ffn'].reshape(1, D), w['w_gu'], w['w_down'], w['g_final'].reshape(1, D))


def _moba_prompt_kernel(slopes_ref, q_ref, kv_ref, o_ref, kvb_ref, km_ref):
    hp = pl.program_id(1)
    i = pl.program_id(2)
    tq = q_ref.shape[0]
    T = kv_ref.shape[-1]
    nb = T // MOBA_BLOCK
    lane = lax.broadcasted_iota(jnp.int32, (1, LANES), 1)

    @pl.when(i == 0)
    def _():
        kvb_ref[...] = kv_ref[0].astype(BF16)
        avg = jnp.where(lax.broadcasted_iota(jnp.int32, (T, LANES), 0) // MOBA_BLOCK
                        == lax.broadcasted_iota(jnp.int32, (T, LANES), 1),
                        1.0 / MOBA_BLOCK, 0.0).astype(BF16)
        for hh in range(2):
            kh, kl = _split(kv_ref[0, 0, hh])
            km_ref[hh] = _dot(kh, avg) + _dot(kl, avg)

    r_c = (lax.broadcasted_iota(jnp.int32, (tq, tq), 0) - lax.broadcasted_iota(jnp.int32, (tq, tq), 1))
    rel = r_c.astype(F32)
    heads = []
    for hh in range(2):
        slope = slopes_ref[hp * 2 + hh]
        q = q_ref[:, hh * HEAD_DIM:(hh + 1) * HEAD_DIM]
        gate = _dot_hilo(q, km_ref[hh])
        gate = jnp.where(lane < i, gate, jnp.where(lane < nb, NEG_INF, REMOVED))
        sel = _topk_pick(gate, MOBA_TOPK, lane.astype(F32))
        heads.append((slope, (q * SCALE).astype(BF16), sel, slope * rel))

    def attend(c):
        n = (c + 1) * MOBA_BLOCK
        outs = []
        for hh, (slope, qs, sel, bias) in enumerate(heads):
            s = _dot(qs, kvb_ref[0, hh, :, :n])
            tiles = []
            for j in range(c):
                sj = s[:, j * MOBA_BLOCK:(j + 1) * MOBA_BLOCK] - (bias + slope * float((c - j) * MOBA_BLOCK))
                tiles.append(jnp.where(sel[:, j:j + 1] > 0.5, sj, NEG_INF))
            tiles.append(jnp.where(r_c >= 0, s[:, c * MOBA_BLOCK:] - bias, NEG_INF))
            m = functools.reduce(jnp.maximum, tiles)
            m = jnp.max(m, axis=-1, keepdims=True)
            es = [jnp.exp(tl - m) for tl in tiles]
            l = jnp.sum(functools.reduce(jnp.add, es), axis=-1, keepdims=True)
            e = jnp.concatenate(es, axis=-1).astype(BF16)
            outs.append(_dot_nt(e, kvb_ref[1, hh, :, :n]) / l)
        o_ref[...] = jnp.concatenate(outs, axis=-1)

    for c in range(nb):
        pl.when(i == c)(functools.partial(attend, c))


def _moba_prompt(q_a, mkv_t, slopes, B, T):
    tq = TQ
    assert tq == MOBA_BLOCK
    nq = T // tq
    hp_n = H_A // 2
    grid_spec = pltpu.PrefetchScalarGridSpec(
        num_scalar_prefetch=0,
        grid=(B, hp_n, nq),
        in_specs=[pl.BlockSpec(memory_space=pltpu.SMEM),
                  pl.BlockSpec((tq, LANES), lambda b, h, i: (b * nq + i, h)),
                  pl.BlockSpec((1, 2, 2, HEAD_DIM, T), lambda b, h, i: (b, 0, h, 0, 0))],
        out_specs=pl.BlockSpec((tq, LANES), lambda b, h, i: (b * nq + i, h)),
        scratch_shapes=[pltpu.VMEM((2, 2, HEAD_DIM, T), BF16), pltpu.VMEM((2, HEAD_DIM, LANES), F32)],
    )
    return pl.pallas_call(
        _moba_prompt_kernel,
        out_shape=jax.ShapeDtypeStruct((B * T, W_A), F32),
        grid_spec=grid_spec,
        compiler_params=_params(("parallel", "parallel", "arbitrary")),
        name="moba_prompt",
    )(slopes, q_a, mkv_t)


def _compress_tokens(read_rows, nseg, c, wbig_ref, perows_ref, b1_ref, w2_ref, b2_ref):
    x = jnp.concatenate([read_rows(pos).astype(BF16) for pos in range(CMP_STRIDE)], axis=-1)
    hid = _dot(jnp.concatenate([x, perows_ref[c].astype(BF16)], axis=0), wbig_ref[c])
    cpe = hid[nseg:nseg + 1, :CMP_HID] + hid[nseg + 1:nseg + 2, CMP_HID:2 * CMP_HID]
    hid = hid[:nseg]
    out = []
    for kv in range(KV_B):
        top = hid[:, kv * 2 * CMP_HID:kv * 2 * CMP_HID + CMP_HID]
        bot = hid[:, kv * 2 * CMP_HID + CMP_HID:(kv + 1) * 2 * CMP_HID]
        bot = jnp.concatenate([bot[1:, :], bot[:1, :]], axis=0)
        h = _gelu_tanh(top + bot + cpe + b1_ref[c])
        out.append(_dot(h.astype(BF16), w2_ref[c]) + b2_ref[c])
    return out


_CW_NAMES = ('wbig', 'perows', 'b1', 'w2', 'b2')


def _compress_prompt_kernel(k_ref, v_ref, wbig_ref, perows_ref, b1_ref, w2_ref, b2_ref, o_ref):
    T = k_ref.shape[0]
    nseg = T // CMP_STRIDE
    res = []
    for c, ref in enumerate((k_ref, v_ref)):
        read = lambda pos, ref=ref: ref[pl.ds(pos, nseg, stride=CMP_STRIDE), :]
        res += _compress_tokens(read, nseg, c, wbig_ref, perows_ref, b1_ref, w2_ref, b2_ref)
    o_ref[0] = jnp.concatenate(res, axis=-1)


def _compress_prompt(kc, vc, cw, B, T):
    nseg = T // CMP_STRIDE
    return pl.pallas_call(
        _compress_prompt_kernel,
        out_shape=jax.ShapeDtypeStruct((B, nseg, 4 * HEAD_DIM), F32),
        grid=(B,),
        in_specs=[pl.BlockSpec((T, W_KVB), lambda b: (b, 0)),
                  pl.BlockSpec((T, W_KVB), lambda b: (b, 0))]
                 + [_const_spec(cw[n].shape) for n in _CW_NAMES],
        out_specs=pl.BlockSpec((1, nseg, 4 * HEAD_DIM), lambda b: (b, 0, 0)),
        compiler_params=_params(("parallel",)),
        name="compress_prompt",
    )(kc, vc, *[cw[n] for n in _CW_NAMES])


def _sel_scores(psum, cur, n_cmp_lanes, nsb_lanes, nsb):
    cn = lax.broadcasted_iota(jnp.int32, (n_cmp_lanes, nsb_lanes), 0) * CMP_STRIDE
    sj = lax.broadcasted_iota(jnp.int32, (n_cmp_lanes, nsb_lanes), 1) * SEL_BLOCK
    overlap = jnp.where((cn < sj + SEL_BLOCK) & (cn + CMP_LEN > sj), 1.0, 0.0).astype(BF16)
    hi, lo = _split(psum)
    imp = _dot(hi, overlap) + _dot(lo, overlap)
    jr = lax.broadcasted_iota(jnp.int32, (1, nsb_lanes), 1)
    forced = (jr == 0) | ((jr <= cur) & (jr > cur - SEL_LOCAL))
    imp = jnp.where(jr <= cur, jnp.where(forced, FORCE_SCORE, imp), jnp.where(jr < nsb, NEG_INF, REMOVED))
    sel = _topk_pick(imp, min(SEL_TOPN, nsb), jr.astype(F32))
    return jnp.where(jr <= cur, sel, 0.0)


def _nsa_prompt_kernel(slopes_ref, q_ref, ckv_ref, sel_ref, win_ref, gn_ref, o_ref,
                       selb_ref, winb_ref, exp_ref):
    i = pl.program_id(1)
    tq = q_ref.shape[0]
    T = sel_ref.shape[-1]
    n_cmp = ckv_ref.shape[1]
    nsb = T // SEL_BLOCK
    nkt = T // tq
    G = GROUP_B

    @pl.when(i == 0)
    def _():
        selb_ref[...] = sel_ref[0].astype(BF16)
        winb_ref[...] = win_ref[0].astype(BF16)
        exp_ref[...] = _block_flags_to_keys(LANES, T, SEL_BLOCK)

    qpos = i * tq + lax.broadcasted_iota(jnp.int32, (tq, 1), 0)
    c_end = lax.broadcasted_iota(jnp.int32, (1, n_cmp), 1) * CMP_STRIDE + (CMP_LEN - 1)
    cmp_mask = (c_end <= qpos) & (c_end < T)
    cur = qpos // SEL_BLOCK
    gn = gn_ref[...]
    groups = []
    for kv in range(KV_B):
        lo, hi = kv * HEAD_DIM, (kv + 1) * HEAD_DIM
        ck = ckv_ref[0, :, lo:hi].astype(BF16)
        cv = ckv_ref[0, :, W_KVB + lo:W_KVB + hi].astype(BF16)
        per_head = []
        psum = jnp.zeros((tq, n_cmp), F32)
        for g in range(G):
            h = kv * G + g
            q = (q_ref[:, h * HEAD_DIM:(h + 1) * HEAD_DIM] * SCALE).astype(BF16)
            e, l = _softmax_parts(_dot_nt(q, ck), cmp_mask)
            p = e / l
            psum = psum + p
            per_head.append((h, q, gn[:, 3 * h:3 * h + 1] * _dot(p.astype(BF16), cv)))
        groups.append((_sel_scores(psum, cur, n_cmp, LANES, nsb).astype(BF16), per_head))

    def attend(c):
        n = (c + 1) * tq
        w0 = max(c - (-(-WINDOW // tq)), 0) * tq
        dist_i = (c * tq + lax.broadcasted_iota(jnp.int32, (tq, n), 0)
                  - lax.broadcasted_iota(jnp.int32, (tq, n), 1))
        dist = dist_i.astype(F32)
        thr = jnp.where(dist_i >= 0, 0.5, 2.0)
        wmask = (dist_i[:, w0:] >= 0) & (dist_i[:, w0:] < WINDOW)
        outs = []
        for kv, (selb, per_head) in enumerate(groups):
            allowed = _dot(selb, exp_ref[:, :n]) > thr
            for h, q, o_cmp in per_head:
                bias = slopes_ref[h] * dist
                s = jnp.where(allowed, _dot(q, selb_ref[0, kv, :, :n]) - bias, NEG_INF)
                e = jnp.exp(s - jnp.max(s, axis=-1, keepdims=True))
                o_sel = _dot_nt(e.astype(BF16), selb_ref[1, kv, :, :n]) / jnp.sum(e, axis=-1, keepdims=True)
                s = jnp.where(wmask, _dot(q, winb_ref[0, kv, :, w0:n]) - bias[:, w0:], NEG_INF)
                e = jnp.exp(s - jnp.max(s, axis=-1, keepdims=True))
                o_win = _dot_nt(e.astype(BF16), winb_ref[1, kv, :, w0:n]) / jnp.sum(e, axis=-1, keepdims=True)
                outs.append(o_cmp + gn[:, 3 * h + 1:3 * h + 2] * o_sel + gn[:, 3 * h + 2:3 * h + 3] * o_win)
        o_ref[...] = jnp.concatenate(outs, axis=-1)

    for c in range(nkt):
        pl.when(i == c)(functools.partial(attend, c))


def _nsa_prompt(q_b, ckv, nkv_t, win_t, gn, slopes, B, T):
    tq = TQ
    nq = T // tq
    assert T // SEL_BLOCK <= LANES
    grid_spec = pltpu.PrefetchScalarGridSpec(
        num_scalar_prefetch=0,
        grid=(B, nq),
        in_specs=[pl.BlockSpec(memory_space=pltpu.SMEM),
                  pl.BlockSpec((tq, W_B), lambda b, i: (b * nq + i, 0)),
                  pl.BlockSpec((1,) + ckv.shape[1:], lambda b, i: (b, 0, 0)),
                  pl.BlockSpec((1, 2, KV_B, HEAD_DIM, T), lambda b, i: (b, 1, 0, 0, 0)),
                  pl.BlockSpec((1, 2, KV_B, HEAD_DIM, T), lambda b, i: (b, 0, 0, 0, 0)),
                  pl.BlockSpec((tq, GN_PAD), lambda b, i: (b * nq + i, 0))],
        out_specs=pl.BlockSpec((tq, W_B), lambda b, i: (b * nq + i, 0)),
        scratch_shapes=[pltpu.VMEM((2, KV_B, HEAD_DIM, T), BF16), pltpu.VMEM((2, KV_B, HEAD_DIM, T), BF16),
                        pltpu.VMEM((LANES, T), BF16)],
    )
    return pl.pallas_call(
        _nsa_prompt_kernel,
        out_shape=jax.ShapeDtypeStruct((B * T, W_B), F32),
        grid_spec=grid_spec,
        compiler_params=_params(("parallel", "arbitrary")),
        name="nsa_prompt",
    )(slopes, q_b, ckv, nkv_t, win_t, gn)


def _block_diag_q(q, rows_per_head, n_heads):
    t, w = q.shape
    qt = jnp.concatenate([q] * n_heads, axis=0)
    r = lax.broadcasted_iota(jnp.int32, (n_heads * t, w), 0) // rows_per_head
    c = lax.broadcasted_iota(jnp.int32, (n_heads * t, w), 1) // HEAD_DIM
    return jnp.where(r == c, qt, 0.0)


def _diag_heads(o, rows_per_head, n_heads):
    return jnp.concatenate(
        [o[h * rows_per_head:(h + 1) * rows_per_head, h * HEAD_DIM:(h + 1) * HEAD_DIM] for h in range(n_heads)],
        axis=0)


def _moba_sample_kernel(pt_ref, slopes_ref, q_ref, new_ref, *refs, past, pps):
    pages = refs[:pps]
    o_ref, qbd_ref, m_ref, l_ref, g_ref, acc_ref = refs[pps:]
    j = pl.program_id(1)
    nstep = pl.num_programs(1)
    bps = pps // 2
    nblk = acc_ref.shape[0]
    t = q_ref.shape[0]
    R = H_A * t
    row_h = lax.broadcasted_iota(jnp.int32, (R, 1), 0) // t
    row_t = lax.broadcasted_iota(jnp.int32, (R, 1), 0) % t
    slope = jnp.zeros((R, 1), F32)
    for h in range(H_A):
        slope = jnp.where(row_h == h, slopes_ref[h], slope)
    qpos = past + row_t
    lane = lax.broadcasted_iota(jnp.int32, (1, LANES), 1)

    @pl.when(j == 0)
    def _():
        qh, ql = _split(_block_diag_q(q_ref[...] * SCALE, t, H_A))
        qbd_ref[...] = jnp.concatenate([qh, ql], axis=0)
        m_ref[...] = jnp.full(m_ref.shape, NEG_INF, F32)
        l_ref[...] = jnp.zeros(l_ref.shape, F32)
        g_ref[...] = jnp.full(g_ref.shape, REMOVED, F32)

    qbd = qbd_ref[...]

    def scores(kt):
        s2 = _dot(qbd, kt)
        return s2[:R] + s2[R:]

    for blk in range(bps):
        jb = j * bps + blk
        p0, p1 = pages[2 * blk], pages[2 * blk + 1]
        s_raw = jnp.concatenate([scores(p[0, 0].reshape(W_A, -1).astype(BF16)) for p in (p0, p1)], axis=-1)
        gate = jnp.sum(s_raw, axis=-1, keepdims=True) * (1.0 / (MOBA_BLOCK * SCALE))
        kpos = jb * MOBA_BLOCK + lax.broadcasted_iota(jnp.int32, (1, MOBA_BLOCK), 1)
        s = s_raw - slope * (qpos - kpos).astype(F32)
        m = jnp.max(s, axis=-1, keepdims=True)
        e = jnp.exp(s - m).astype(BF16)
        half = MOBA_BLOCK // 2
        o = (_dot_nt(e[:, :half], p0[0, 1].reshape(W_A, -1).astype(BF16))
             + _dot_nt(e[:, half:], p1[0, 1].reshape(W_A, -1).astype(BF16)))
        acc_ref[jb] = o
        m_ref[...] = jnp.where(lane == jb, m, m_ref[...])
        l_ref[...] = jnp.where(lane == jb, jnp.sum(e.astype(F32), axis=-1, keepdims=True), l_ref[...])
        g_ref[...] = jnp.where(lane == jb, gate, g_ref[...])

    @pl.when(j == nstep - 1)
    def _():
        sel = _topk_pick(g_ref[...], MOBA_TOPK, lane.astype(F32))
        sel = jnp.where(lane < nblk, sel, 0.0) > 0.5
        kn = new_ref[:, :W_A].astype(BF16)
        vn = new_ref[:, W_A:].astype(BF16)
        npos = past + lax.broadcasted_iota(jnp.int32, (1, t), 1)
        s2 = _dot_nt(qbd, kn)
        s_own = s2[:R] + s2[R:] - slope * (qpos - npos).astype(F32)
        own_mask = npos <= qpos
        s_own = jnp.where(own_mask, s_own, NEG_INF)
        m_own = jnp.max(s_own, axis=-1, keepdims=True)
        m_all = m_ref[...]
        m_tot = jnp.maximum(jnp.max(jnp.where(sel, m_all, NEG_INF), axis=-1, keepdims=True), m_own)
        e_own = jnp.where(own_mask, jnp.exp(s_own - m_tot), 0.0)
        wgt = jnp.where(sel, jnp.exp(m_all - m_tot), 0.0)
        l_tot = jnp.sum(wgt * l_ref[...], axis=-1, keepdims=True) + jnp.sum(e_own, axis=-1, keepdims=True)
        o = _dot(e_own.astype(BF16), vn)
        for b in range(nblk):
            o = o + wgt[:, b:b + 1] * acc_ref[b]
        o = _diag_heads(o, t, H_A) / jnp.maximum(l_tot, 1e-30)
        for h in range(H_A):
            o_ref[:, h * HEAD_DIM:(h + 1) * HEAD_DIM] = o[h * t:(h + 1) * t, :]


def _page_specs(block, pps):
    nd = len(block)
    return [pl.BlockSpec(block, lambda b, j, pt, r=r: (pt[b, pps * j + r],) + (0,) * (nd - 1))
            for r in range(pps)]


def _moba_sample(q_a, moba_new, cache_t, page_table, slopes, DB, t):
    n_pool, _, _, _, page = cache_t.shape
    n_pages = page_table.shape[1]
    past = n_pages * page
    pps = PAGES_PER_STEP
    assert page * 2 == MOBA_BLOCK and past % MOBA_BLOCK == 0 and t <= MOBA_BLOCK
    assert pps % 2 == 0 and n_pages % pps == 0
    nblk = n_pages // 2
    assert nblk <= LANES
    R = H_A * t
    grid_spec = pltpu.PrefetchScalarGridSpec(
        num_scalar_prefetch=1,
        grid=(DB, n_pages // pps),
        in_specs=[pl.BlockSpec(memory_space=pltpu.SMEM),
                  pl.BlockSpec((t, W_A), lambda b, j, pt: (b, 0)),
                  pl.BlockSpec((t, 2 * W_A), lambda b, j, pt: (b, 0))]
                 + _page_specs((1, 2, H_A, HEAD_DIM, page), pps),
        out_specs=pl.BlockSpec((t, W_A), lambda b, j, pt: (b, 0)),
        scratch_shapes=[pltpu.VMEM((2 * R, W_A), BF16),
                        pltpu.VMEM((R, LANES), F32), pltpu.VMEM((R, LANES), F32), pltpu.VMEM((R, LANES), F32),
                        pltpu.VMEM((nblk, R, W_A), F32)],
    )
    return pl.pallas_call(
        functools.partial(_moba_sample_kernel, past=past, pps=pps),
        out_shape=jax.ShapeDtypeStruct((DB * t, W_A), F32),
        grid_spec=grid_spec,
        compiler_params=_params(("parallel", "arbitrary")),
        name="moba_sample",
    )(page_table, slopes, q_a, moba_new, *([cache_t] * pps))


def _nsa_sample_kernel(pt_ref, slopes_ref, q_ref, new_ref, wnew_ref, gn_ref, wcache_ref, *refs, past, pps):
    pages = refs[:pps]
    (wbig_ref, perows_ref, b1_ref, w2_ref, b2_ref,
     o_ref, wout_ref, cmpk_ref, cmpv_ref, sel_ref) = refs[pps:]
    j = pl.program_id(1)
    nstep = pl.num_programs(1)
    page = pages[0].shape[-1]
    t = q_ref.shape[0]
    L = past + t
    nseg = cmpk_ref.shape[1]
    n_cmp = -(-L // CMP_STRIDE) - 1
    nsb = -(-L // SEL_BLOCK)
    nsb_lanes = -(-nsb // LANES) * LANES
    lsel = sel_ref.shape[0]
    spp = page // CMP_STRIDE

    pr = lax.broadcasted_iota(jnp.int32, (page, page), 0)
    pc = lax.broadcasted_iota(jnp.int32, (page, page), 1)
    perm = jnp.where(pc == (pr % spp) * CMP_STRIDE + pr // spp, 1.0, 0.0).astype(BF16)
    ident = jnp.where(pc == pr, 1.0, 0.0).astype(BF16)
    for r in range(pps):
        pidx = pps * j + r
        pg = pages[r][0].reshape(4 * W_KVB, page).astype(BF16)
        cm = _dot_nt(perm, pg[:2 * W_KVB])
        seg0 = pl.multiple_of(pidx * spp, spp)
        for pos in range(CMP_STRIDE):
            cmpk_ref[pos, pl.ds(seg0, spp), :] = cm[pos * spp:(pos + 1) * spp, :W_KVB]
            cmpv_ref[pos, pl.ds(seg0, spp), :] = cm[pos * spp:(pos + 1) * spp, W_KVB:]
        sel_ref[pl.ds(pl.multiple_of(pidx * page, page), page), :] = _dot_nt(ident, pg[2 * W_KVB:]).astype(BF16)

    @pl.when(j == nstep - 1)
    def _():
        n_cached = past // CMP_STRIDE
        for ref, lo in ((cmpk_ref, 0), (cmpv_ref, W_KVB)):
            ref[:, n_cached:, :] = jnp.zeros((CMP_STRIDE, nseg - n_cached, W_KVB), F32)
            for pos in range(t):
                ref[pos, n_cached:n_cached + 1, :] = new_ref[pos:pos + 1, lo:lo + W_KVB]
        sel_ref[pl.ds(past, lsel - past), :] = jnp.concatenate(
            [new_ref[:, 2 * W_KVB:], jnp.zeros((lsel - past - t, 2 * W_KVB), F32)], axis=0).astype(BF16)

        toks = []
        for c, ref in enumerate((cmpk_ref, cmpv_ref)):
            read = lambda pos, ref=ref: ref[pos]
            toks += _compress_tokens(read, nseg, c, wbig_ref, perows_ref, b1_ref, w2_ref, b2_ref)

        R = GROUP_B * t
        row_t = lax.broadcasted_iota(jnp.int32, (R, 1), 0) % t
        row_g = lax.broadcasted_iota(jnp.int32, (R, 1), 0) // t
        qpos = past + row_t
        c_end = lax.broadcasted_iota(jnp.int32, (1, nseg), 1) * CMP_STRIDE + (CMP_LEN - 1)
        cmp_mask = (c_end <= qpos) & (lax.broadcasted_iota(jnp.int32, (1, nseg), 1) < n_cmp)
        kpos = lax.broadcasted_iota(jnp.int32, (1, lsel), 1)
        dist = (qpos - kpos).astype(F32)
        causal = (kpos <= qpos) & (kpos < L)
        chunk = LANES * SEL_BLOCK
        expand = _block_flags_to_keys(LANES, chunk, SEL_BLOCK)
        n_buf = wcache_ref.shape[-1]
        wk = jnp.concatenate([wcache_ref[0].reshape(2 * W_KVB, n_buf).T, wnew_ref[...]], axis=0)
        wpos = past - n_buf + lax.broadcasted_iota(jnp.int32, (1, n_buf + t), 1)
        wdist = (qpos - wpos).astype(F32)
        wmask = (wpos <= qpos) & (wpos > qpos - WINDOW) & (wpos >= 0)
        wout_ref[0] = wk[t:, :].T.reshape(wout_ref.shape[1:])
        gn = gn_ref[...]
        for kv in range(KV_B):
            lo, hi = kv * HEAD_DIM, (kv + 1) * HEAD_DIM
            q = jnp.concatenate([q_ref[:, (kv * GROUP_B + g) * HEAD_DIM:(kv * GROUP_B + g + 1) * HEAD_DIM]
                                 for g in range(GROUP_B)], axis=0)
            q = (q * SCALE).astype(BF16)
            slope = jnp.zeros((R, 1), F32)
            for g in range(GROUP_B):
                slope = jnp.where(row_g == g, slopes_ref[kv * GROUP_B + g], slope)
            ck = toks[kv].astype(BF16)
            cv = toks[KV_B + kv].astype(BF16)
            e, l = _softmax_parts(_dot_nt(q, ck), cmp_mask)
            p = e / l
            o_cmp = _dot(p.astype(BF16), cv)
            psum = p[0:t]
            for g in range(1, GROUP_B):
                psum = psum + p[g * t:(g + 1) * t]
            sel = _sel_scores(psum, qpos[0:t] // SEL_BLOCK, nseg, nsb_lanes, nsb)
            selb = jnp.concatenate([sel] * GROUP_B, axis=0).astype(BF16)
            allowed = jnp.concatenate(
                [_dot(selb[:, ci * LANES:(ci + 1) * LANES], expand[:, :min(chunk, lsel - ci * chunk)])
                 for ci in range(nsb_lanes // LANES)], axis=-1)
            allowed = (allowed > 0.5) & causal
            ks = sel_ref[:, lo:hi]
            vs = sel_ref[:, W_KVB + lo:W_KVB + hi]
            e, l = _softmax_parts(_dot_nt(q, ks) - slope * dist, allowed)
            o_sel = _dot(e.astype(BF16), vs) / l
            e, l = _softmax_parts(_dot_nt(q, wk[:, lo:hi].astype(BF16)) - slope * wdist, wmask)
            o_win = _dot(e.astype(BF16), wk[:, W_KVB + lo:W_KVB + hi].astype(BF16)) / l
            for g in range(GROUP_B):
                h = kv * GROUP_B + g
                rs = slice(g * t, (g + 1) * t)
                o_ref[:, h * HEAD_DIM:(h + 1) * HEAD_DIM] = (
                    gn[:, 3 * h:3 * h + 1] * o_cmp[rs] + gn[:, 3 * h + 1:3 * h + 2] * o_sel[rs]
                    + gn[:, 3 * h + 2:3 * h + 3] * o_win[rs])


def _nsa_sample(q_b, nsa_new, win_new, gn, win_cache_t, cache_t, page_table, cw, slopes, DB, t):
    n_pool, _, _, _, page = cache_t.shape
    n_pages = page_table.shape[1]
    past = n_pages * page
    n_buf = win_cache_t.shape[-1]
    pps = PAGES_PER_STEP
    assert past % SEL_BLOCK == 0 and past % CMP_STRIDE == 0 and n_pages % pps == 0 and t <= CMP_STRIDE
    nseg = past // CMP_STRIDE + 1
    nseg_pad = -(-nseg // 16) * 16
    lsel = past + SEL_BLOCK
    wblock = (1,) + win_cache_t.shape[1:]
    grid_spec = pltpu.PrefetchScalarGridSpec(
        num_scalar_prefetch=1,
        grid=(DB, n_pages // pps),
        in_specs=[pl.BlockSpec(memory_space=pltpu.SMEM),
                  pl.BlockSpec((t, W_B), lambda b, j, pt: (b, 0)),
                  pl.BlockSpec((t, 4 * W_KVB), lambda b, j, pt: (b, 0)),
                  pl.BlockSpec((t, 2 * W_KVB), lambda b, j, pt: (b, 0)),
                  pl.BlockSpec((t, GN_PAD), lambda b, j, pt: (b, 0)),
                  pl.BlockSpec(wblock, lambda b, j, pt: (b, 0, 0, 0, 0))]
                 + _page_specs((1, 4, KV_B, HEAD_DIM, page), pps)
                 + [_const_spec(cw[n].shape) for n in _CW_NAMES],
        out_specs=[pl.BlockSpec((t, W_B), lambda b, j, pt: (b, 0)),
                   pl.BlockSpec(wblock, lambda b, j, pt: (b, 0, 0, 0, 0))],
        scratch_shapes=[pltpu.VMEM((CMP_STRIDE, nseg_pad, W_KVB), F32),
                        pltpu.VMEM((CMP_STRIDE, nseg_pad, W_KVB), F32),
                        pltpu.VMEM((lsel, 2 * W_KVB), BF16)],
    )
    return pl.pallas_call(
        functools.partial(_nsa_sample_kernel, past=past, pps=pps),
        out_shape=[jax.ShapeDtypeStruct((DB * t, W_B), F32),
                   jax.ShapeDtypeStruct(win_cache_t.shape, F32)],
        grid_spec=grid_spec,
        compiler_params=_params(("parallel", "arbitrary")),
        name="nsa_sample",
    )(page_table, slopes, q_b, nsa_new, win_new, gn, win_cache_t, *([cache_t] * pps),
      *[cw[n] for n in _CW_NAMES])


def _alibi_slopes(n):
    return jnp.exp2(-8.0 * (jnp.arange(n, dtype=F32) + 1.0) / n)


def _prep_weights(w_in, cmp_pe, cmp_w1, cmp_b1, cmp_w2, cmp_b2):
    gn = w_in[:, _C_GA:_C_GA + 3 * H_B]
    w_in_r = jnp.concatenate(
        [w_in[:, :_C_GA], w_in[:, _C_GA + 3 * H_B:],
         gn, jnp.zeros((D_MODEL, GN_PAD - 3 * H_B), w_in.dtype)], axis=1).astype(BF16)
    w_kvt = jnp.concatenate([w_in[:, _C_MKV:_C_QB], w_in[:, _C_NKV:_C_GA]], axis=1).T.astype(BF16)
    half = CMP_STRIDE * HEAD_DIM
    wcat = jnp.concatenate([cmp_w1[:, :half].reshape(2, CMP_STRIDE, HEAD_DIM, CMP_HID),
                            cmp_w1[:, half:].reshape(2, CMP_STRIDE, HEAD_DIM, CMP_HID)], axis=-1)
    wbig = jnp.einsum('cpdh,kl->cpkdlh', wcat, jnp.eye(KV_B, dtype=wcat.dtype))
    wbig = wbig.reshape(2, CMP_STRIDE * KV_B * HEAD_DIM, KV_B * 2 * CMP_HID).astype(BF16)
    pe4 = cmp_pe.reshape(2, 2, CMP_STRIDE, 1, HEAD_DIM)
    perows = jnp.broadcast_to(pe4, (2, 2, CMP_STRIDE, KV_B, HEAD_DIM)).reshape(2, 2, -1)
    perows = jnp.concatenate([perows, jnp.zeros((2, 14, perows.shape[-1]), perows.dtype)], axis=1)
    cw = {'wbig': wbig, 'perows': perows, 'b1': cmp_b1.reshape(2, 1, CMP_HID),
          'w2': cmp_w2.astype(BF16), 'b2': cmp_b2.reshape(2, 1, HEAD_DIM)}
    return w_in_r, w_kvt, cw


def _token_minor(x):
    return jnp.transpose(x, (0, 2, 3, 4, 1))


def _token_major(x):
    return jnp.transpose(x, (0, 4, 1, 2, 3))


def kernel(x_prompt, x_sample, cache_moba_kv, cache_nsa_kv, cache_nsa_win, page_table, c_prompt, c_sample,
           w_ada, b_ada, g_attn, w_in, cmp_pe, cmp_w1, cmp_b1, cmp_w2, cmp_b2, w_pa, w_pb, w_out,
           g_ffn, w_gu, w_down, g_final):
    B, T, D = x_prompt.shape
    DB, t, _ = x_sample.shape
    assert T % TQ == 0 and T >= WINDOW

    w_in_r, w_kvt, cw = _prep_weights(w_in, cmp_pe, cmp_w1, cmp_b1, cmp_w2, cmp_b2)
    wo = {'w_pa': w_pa.astype(BF16), 'w_pb': w_pb.astype(BF16), 'w_out': w_out.astype(BF16),
          'w_gu': w_gu.astype(BF16), 'w_down': w_down.astype(BF16), 'g_ffn': g_ffn, 'g_final': g_final}
    slopes_a = _alibi_slopes(H_A)
    slopes_b = _alibi_slopes(H_B)

    mod = _ada(jnp.concatenate([c_prompt, c_sample], axis=0), w_ada.astype(BF16), b_ada)
    mod = mod.reshape(B + DB, N_ADA, D)
    mod_p, mod_s = mod[:B], mod[B:]

    tm = 512
    xp3 = x_prompt.reshape(B * T // tm, tm, D)
    qa, qb, ga, gb, gn, kc, vc, mkv_t, nkv_t, win_t = _proj_in(
        xp3, mod_p, g_attn, w_in_r, T // tm, 1, w_kvt=w_kvt, batch_len=T)
    o_a = _moba_prompt(qa, mkv_t, slopes_a, B, T)
    ckv = _compress_prompt(kc, vc, cw, B, T)
    o_b = _nsa_prompt(qb, ckv, nkv_t, win_t, gn, slopes_b, B, T)
    tm_o = 256
    y_prompt = _out_stage(x_prompt.reshape(B * T // tm_o, tm_o, D), mod_p, o_a, o_b, ga, gb, wo,
                          T // tm_o, 1).reshape(B, T, D)
    moba_kv_prompt = _token_major(mkv_t)
    nsa_kv_prompt = _token_major(nkv_t)
    nsa_win_prompt = _token_major(win_t[..., T - min(WINDOW, T):])

    qa, qb, ga, gb, gn, mkv, nkv, win = _proj_in(x_sample, mod_s, g_attn, w_in_r, 1, DB)
    o_a = _moba_sample(qa, mkv, _token_minor(cache_moba_kv), page_table, slopes_a, DB, t)
    o_b, win_s = _nsa_sample(qb, nkv, win, gn, _token_minor(cache_nsa_win), _token_minor(cache_nsa_kv),
                             page_table, cw, slopes_b, DB, t)
    y_sample = _out_stage(x_sample, mod_s, o_a, o_b, ga, gb, wo, 1, DB)
    moba_kv_sample = mkv.reshape(DB, t, 2, H_A, HEAD_DIM)
    nsa_kv_sample = nkv.reshape(DB, t, 4, KV_B, HEAD_DIM)
    nsa_win_sample = _token_major(win_s)

    return (y_prompt, y_sample, moba_kv_prompt, nsa_kv_prompt, nsa_win_prompt,
            moba_kv_sample, nsa_kv_sample, nsa_win_sample)
```
